```python
import jax, jax.numpy as jnp
from jax import lax
import numpy as np

D_MODEL = 1024
BATCH = 4
SEQ = 8192
DEPTH = 1

CHUNK = 64
Q_BLOCK = 128
EPS = 1e-6
MLA_HEADS = 8
MLA_NOPE = 64
MLA_ROPE = 32
MLA_V = 64
MLA_Q_RANK = 384
MLA_KV_RANK = 256
ROPE_BASE = 10000.0
MLA_WIDTH = MLA_HEADS * MLA_V
DSA_HEADS = 8
DSA_HEAD_DIM = 64
DSA_WIDTH = DSA_HEADS * DSA_HEAD_DIM
IDX_HEADS = 8
IDX_DIM = 32
TOPK_MAX = 256
REL_BUCKETS = 32
REL_MAX_DIST = 128

IN_SPLITS = (MLA_Q_RANK, MLA_KV_RANK, MLA_ROPE, MLA_WIDTH,
             DSA_WIDTH, DSA_WIDTH, DSA_WIDTH, DSA_WIDTH,
             IDX_HEADS * IDX_DIM, IDX_DIM, IDX_HEADS,
             D_MODEL, D_MODEL)
IN_TOTAL = sum(IN_SPLITS)

kernel_name = 'hybrid_mla_dsa_gated_parallel'


def rmsnorm(x, g):
    xf = x.astype(jnp.float32)
    y = xf * lax.rsqrt(jnp.mean(xf * xf, axis=-1, keepdims=True) + EPS)
    return (y * g.astype(jnp.float32)).astype(x.dtype)


def rope(x, pos):
    half = x.shape[-1] // 2
    freqs = ROPE_BASE ** (-jnp.arange(half, dtype=jnp.float32) / half)
    ang = pos.astype(jnp.float32)[:, None] * freqs[None, :]
    cos = jnp.cos(ang)[None, :, None, :].astype(x.dtype)
    sin = jnp.sin(ang)[None, :, None, :].astype(x.dtype)
    x1, x2 = x[..., :half], x[..., half:]
    return jnp.concatenate([x1 * cos - x2 * sin, x1 * sin + x2 * cos], axis=-1)


def t5_bucket(rel):
    nb = REL_BUCKETS // 2
    max_exact = nb // 2
    ret = (rel > 0).astype(jnp.int32) * nb
    n = jnp.abs(rel)
    nf = jnp.maximum(n, 1).astype(jnp.float32)
    large = max_exact + (jnp.log(nf / max_exact) / np.log(REL_MAX_DIST / max_exact)
                         * (nb - max_exact)).astype(jnp.int32)
    large = jnp.minimum(large, nb - 1)
    return ret + jnp.where(n < max_exact, n, large)


def to_blocks(a):
    return jnp.moveaxis(a.reshape((a.shape[0], -1, Q_BLOCK) + a.shape[2:]), 1, 0)


def from_blocks(a):
    a = jnp.moveaxis(a, 0, 1)
    return a.reshape((a.shape[0], -1) + a.shape[3:])


def mla_attention(q_nope, q_pe, k_nope, k_pe, v, pos):
    scale = (MLA_NOPE + MLA_ROPE) ** -0.5
    key_chunk = pos // CHUNK

    def block(args):
        qn, qp, qpos = args
        logits = (jnp.einsum('bqhd,bkhd->bhqk', qn, k_nope)
                  + jnp.einsum('bqhr,bkr->bhqk', qp, k_pe))
        logits = logits.astype(jnp.float32) * scale
        mask = key_chunk[None, :] <= (qpos // CHUNK)[:, None]
        logits = jnp.where(mask[None, None], logits, -jnp.inf)
        p = jax.nn.softmax(logits, axis=-1).astype(v.dtype)
        return jnp.einsum('bhqk,bkhd->bqhd', p, v)

    out = lax.map(block, (to_blocks(q_nope), to_blocks(q_pe), pos.reshape(-1, Q_BLOCK)))
    return from_blocks(out)


def dsa_attention(q, k, v, q_idx, k_idx, w_idx, rel_bias, pos, topk):
    scale = DSA_HEAD_DIM ** -0.5
    key_chunk = pos // CHUNK
    gather = jax.vmap(lambda src, ids: src[ids])

    def block(args):
        qb, qi, wi, qpos = args
        qchunk = qpos // CHUNK
        s = jnp.einsum('bqhd,bkd->bqhk', qi, k_idx).astype(jnp.float32) * (IDX_DIM ** -0.5)
        score = jnp.einsum('bqhk,bqh->bqk', jax.nn.relu(s),
                           wi.astype(jnp.float32) * (IDX_HEADS ** -0.5))
        admissible = key_chunk[None, :] <= qchunk[:, None]
        score = jnp.where(admissible[None], score, -jnp.inf)
        _, idx = lax.top_k(score, topk)
        valid = (idx // CHUNK) <= qchunk[None, :, None]
        k_sel = gather(k, idx)
        v_sel = gather(v, idx)
        logits = jnp.einsum('bqhd,bqkhd->bhqk', qb, k_sel).astype(jnp.float32) * scale
        bias = rel_bias[t5_bucket(idx - qpos[None, :, None])]
        logits = logits + jnp.transpose(bias, (0, 3, 1, 2)).astype(jnp.float32)
        logits = jnp.where(valid[:, None], logits, -jnp.inf)
        p = jax.nn.softmax(logits, axis=-1).astype(v.dtype)
        return jnp.einsum('bhqk,bqkhd->bqhd', p, v_sel)

    out = lax.map(block, (to_blocks(q), to_blocks(q_idx), to_blocks(w_idx),
                          pos.reshape(-1, Q_BLOCK)))
    return from_blocks(out)


def setup_inputs(seed: int = 0) -> dict:
    key = jax.random.key(seed)
    ks = jax.random.split(key, 13)

    def dense(k, shape):
        return jax.random.normal(k, shape, jnp.float32) * shape[-2] ** -0.5

    def gain(k, shape):
        return 1.0 + 0.05 * jax.random.normal(k, shape, jnp.float32)

    return {
        'x': jax.random.normal(ks[0], (BATCH, SEQ, D_MODEL), jnp.float32),
        'norm_g': gain(ks[1], (DEPTH, D_MODEL)),
        'w_in': dense(ks[2], (DEPTH, D_MODEL, IN_TOTAL)),
        'g_q_lat': gain(ks[3], (DEPTH, MLA_Q_RANK)),
        'w_uq': dense(ks[4], (DEPTH, MLA_Q_RANK, MLA_HEADS * (MLA_NOPE + MLA_ROPE))),
        'g_kv_lat': gain(ks[5], (DEPTH, MLA_KV_RANK)),
        'w_ukv': dense(ks[6], (DEPTH, MLA_KV_RANK, MLA_HEADS * (MLA_NOPE + MLA_V))),
        'w_o_a': dense(ks[7], (DEPTH, MLA_WIDTH, D_MODEL)),
        'w_o_b': dense(ks[8], (DEPTH, DSA_WIDTH, D_MODEL)),
        'w_out': dense(ks[9], (DEPTH, D_MODEL, D_MODEL)),
        'rel_bias': 0.5 * jax.random.normal(ks[10], (REL_BUCKETS, DSA_HEADS), jnp.float32),
        'final_g': gain(ks[11], (D_MODEL,)),
    }


def reference(x, norm_g, w_in, g_q_lat, w_uq, g_kv_lat, w_ukv, w_o_a, w_o_b, w_out,
              rel_bias, final_g):
    B, S, _ = x.shape
    pos = jnp.arange(S, dtype=jnp.int32)
    topk = min(TOPK_MAX, S // 4)
    cuts = np.cumsum(IN_SPLITS)[:-1].tolist()
    for l in range(DEPTH):
        h = rmsnorm(x, norm_g[l])
        (q_lat, c_kv, k_rope, z_a, q_b, k_b, v_b, z_b,
         q_idx, k_idx, w_idx, gate_a, gate_b) = jnp.split(h @ w_in[l], cuts, axis=-1)

        q = (rmsnorm(q_lat, g_q_lat[l]) @ w_uq[l]).reshape(B, S, MLA_HEADS, MLA_NOPE + MLA_ROPE)
        q_nope, q_pe = q[..., :MLA_NOPE], rope(q[..., MLA_NOPE:], pos)
        kv = (rmsnorm(c_kv, g_kv_lat[l]) @ w_ukv[l]).reshape(B, S, MLA_HEADS, MLA_NOPE + MLA_V)
        k_nope, v_a = kv[..., :MLA_NOPE], kv[..., MLA_NOPE:]
        k_pe = rope(k_rope[:, :, None, :], pos)[:, :, 0, :]
        y_a = mla_attention(q_nope, q_pe, k_nope, k_pe, v_a, pos).reshape(B, S, MLA_WIDTH)
        y_a = y_a * jax.nn.silu(z_a)

        y_b = dsa_attention(q_b.reshape(B, S, DSA_HEADS, DSA_HEAD_DIM),
                            k_b.reshape(B, S, DSA_HEADS, DSA_HEAD_DIM),
                            v_b.reshape(B, S, DSA_HEADS, DSA_HEAD_DIM),
                            q_idx.reshape(B, S, IDX_HEADS, IDX_DIM), k_idx, w_idx,
                            rel_bias, pos, topk).reshape(B, S, DSA_WIDTH)
        y_b = y_b * jax.nn.silu(z_b)

        merged = (jax.nn.sigmoid(gate_a) * (y_a @ w_o_a[l])
                  + jax.nn.sigmoid(gate_b) * (y_b @ w_o_b[l]))
        x = x + merged @ w_out[l]
    return rmsnorm(x, final_g)
```

```python
import functools

import numpy as np
import jax
import jax.numpy as jnp
from jax import lax
from jax.experimental import pallas as pl
from jax.experimental.pallas import tpu as pltpu

D_MODEL = 1024
CHUNK = 64
EPS = 1e-6
MLA_HEADS = 8
MLA_NOPE = 64
MLA_ROPE = 32
MLA_V = 64
MLA_Q_RANK = 384
MLA_KV_RANK = 256
ROPE_BASE = 10000.0
MLA_WIDTH = MLA_HEADS * MLA_V
DSA_HEADS = 8
DSA_HEAD_DIM = 64
DSA_WIDTH = DSA_HEADS * DSA_HEAD_DIM
IDX_HEADS = 8
IDX_DIM = 32
TOPK_MAX = 256
REL_BUCKETS = 32
REL_MAX_DIST = 128
IN_SPLITS = (MLA_Q_RANK, MLA_KV_RANK, MLA_ROPE, MLA_WIDTH,
             DSA_WIDTH, DSA_WIDTH, DSA_WIDTH, DSA_WIDTH,
             IDX_HEADS * IDX_DIM, IDX_DIM, IDX_HEADS,
             D_MODEL, D_MODEL)

LANES = 128
TILE = 256
VMEM_LIMIT = 56 * 1024 * 1024
MXU_DTYPE = jnp.bfloat16
F32 = jnp.float32
NEG_INF = float("-inf")
M_INIT = -1e30

KEY_LO = int(np.int32(np.uint32(0x80800000)))
KEY_HI = 0x7F800001


def _nt(a, b):
    return lax.dot_general(a, b, (((1,), (1,)), ((), ())), preferred_element_type=F32)


def _nn(a, b):
    return jnp.dot(a, b, preferred_element_type=F32)


def _sigmoid(v):
    return 1.0 / (1.0 + jnp.exp(-v))


def _rms(v, g):
    return v * lax.rsqrt(jnp.mean(v * v, axis=-1, keepdims=True) + EPS) * g


def _proj_kernel(x_ref, ng_ref, w1_ref, w2t_ref, gq_ref, wuqm_ref, wuqs_ref, gkv_ref, wk_ref,
                 wvt_ref, cqt_ref, sqt_ref, ck_ref, sk_ref,
                 qat_ref, ka_ref, vat_ref, za_ref, qbt_ref, kb_ref, vbt_ref, zb_ref,
                 qit_ref, ki_ref, wit_ref, ga_ref, gb_ref):
    hb = _rms(x_ref[0], ng_ref[...]).astype(MXU_DTYPE)

    def cols(lo, hi):
        return _nn(hb, w1_ref[:, lo:hi])

    def rows(lo, hi):
        return _nt(w2t_ref[lo:hi, :], hb)

    qn = _rms(cols(0, 384), gq_ref[...]).astype(MXU_DTYPE)
    qm = _nt(wuqm_ref[...], qn)
    qs = _nt(wuqs_ref[...], qn)
    cqt = cqt_ref[...]
    sqt = sqt_ref[...]
    for h in range(MLA_HEADS):
        sl = slice(LANES * h, LANES * (h + 1))
        qat_ref[0, sl, :] = (qm[sl] * cqt + qs[sl] * sqt).astype(MXU_DTYPE)

    kvn = _rms(cols(384, 640), gkv_ref[...]).astype(MXU_DTYPE)
    kn = _nn(kvn, wk_ref[...])
    kpe = cols(640, 768) * ck_ref[...] + cols(768, 896) * sk_ref[...]
    for h in range(MLA_HEADS):
        sl = slice(LANES * h, LANES * (h + 1))
        ka_ref[0, :, sl] = (kn[:, sl] + kpe).astype(MXU_DTYPE)
    vat_ref[0, 0] = _nt(wvt_ref[...], kvn).astype(MXU_DTYPE)

    za = cols(896, 1408)
    za_ref[0] = (za * _sigmoid(za)).astype(MXU_DTYPE)
    kb_ref[0] = cols(1408, 1920).astype(MXU_DTYPE)
    zb = cols(1920, 2432)
    zb_ref[0] = (zb * _sigmoid(zb)).astype(MXU_DTYPE)
    ki_ref[0] = cols(2432, 2560).astype(MXU_DTYPE)
    ga_ref[0] = _sigmoid(cols(2560, 3584)).astype(MXU_DTYPE)
    gb_ref[0] = _sigmoid(cols(3584, 4608)).astype(MXU_DTYPE)

    qbt_ref[0] = (rows(0, 512) * (DSA_HEAD_DIM ** -0.5)).astype(MXU_DTYPE)
    vbt_ref[0, 0] = rows(512, 1024).astype(MXU_DTYPE)
    qit_ref[0] = rows(1024, 1280).astype(MXU_DTYPE)
    wit_ref[0] = rows(1280, 1296)[:IDX_HEADS] * ((IDX_DIM * IDX_HEADS) ** -0.5)


def _diag_admissible():
    key = lax.broadcasted_iota(jnp.int32, (TILE, TILE), 0)
    qry = lax.broadcasted_iota(jnp.int32, (TILE, TILE), 1)
    return (key // CHUNK) <= (qry // CHUNK)


def _mla_kernel(qt_ref, k_ref, vt_ref, z_ref, o_ref):
    i = pl.program_id(2)
    adm = _diag_admissible()
    outs = []
    for hh in range(2):
        sl = slice(LANES * hh, LANES * (hh + 1))
        q_t = qt_ref[0, sl, :]

        def step(j, carry, masked, sl=sl, q_t=q_t):
            m, l, acc = carry
            off = pl.multiple_of(j * TILE, TILE)
            s = _nn(k_ref[0, pl.ds(off, TILE), sl], q_t)
            if masked:
                s = jnp.where(adm, s, NEG_INF)
            m_new = jnp.maximum(m, jnp.max(s, axis=0, keepdims=True))
            p = jnp.exp(s - m_new)
            alpha = jnp.exp(m - m_new)
            l = alpha * l + jnp.sum(p, axis=0, keepdims=True)
            acc = alpha * acc + _nn(vt_ref[0, j], p.astype(MXU_DTYPE))
            return m_new, l, acc

        init = (jnp.full((1, TILE), M_INIT, F32), jnp.zeros((1, TILE), F32),
                jnp.zeros((LANES, TILE), F32))
        carry = lax.fori_loop(0, i, functools.partial(step, masked=False), init)
        _, l, acc = step(i, carry, True)
        outs.append(acc / l)
    y_t = jnp.concatenate([outs[0][:MLA_V], outs[1][MLA_V:]], axis=0)
    o_ref[0] = (y_t.T * z_ref[0].astype(F32)).astype(MXU_DTYPE)


def _key_to_float(k):
    bits = k ^ ((k >> 31) & 0x7FFFFFFF)
    return lax.bitcast_convert_type(bits, F32)


def _any(mask):
    return jnp.max(jnp.where(mask, 1.0, 0.0)) > 0.5


def _dsa_kernel(topk, qbt_ref, kb_ref, vbt_ref, qit_ref, ki_ref, wit_ref, zb_ref, bias_ref, o_ref,
                sc_ref, qm_ref, qim_ref, m_ref, l_ref, acc_ref):
    i = pl.program_id(1)
    adm = _diag_admissible()
    sub = 64
    n_sub = (i + 1) * (TILE // sub)

    row = lax.broadcasted_iota(jnp.int32, (LANES, TILE), 0)
    for h in range(DSA_HEADS):
        g = h // 2
        qm_ref[h] = jnp.where((row // DSA_HEAD_DIM) == (h % 2),
                              qbt_ref[0, LANES * g:LANES * (g + 1), :], 0).astype(MXU_DTYPE)
        gi = h // 4
        qim_ref[h] = jnp.where((row // IDX_DIM) == (h % 4),
                               qit_ref[0, LANES * gi:LANES * (gi + 1), :], 0).astype(MXU_DTYPE)

    def score_block(j, masked):
        off = pl.multiple_of(j * TILE, TILE)
        kij = ki_ref[0, pl.ds(off, TILE), :]
        tot = jnp.zeros((TILE, TILE), F32)
        for h in range(IDX_HEADS):
            s = _nn(kij, qim_ref[h])
            tot = tot + jnp.maximum(s, 0.0) * wit_ref[0, h:h + 1, :]
        if masked:
            tot = jnp.where(adm, tot, NEG_INF)
        sc_ref[pl.ds(off, TILE), :] = tot

    def p1(j, c):
        score_block(j, False)
        return c

    lax.fori_loop(0, i, p1, 0)
    score_block(i, True)

    def count(pred):
        def body(c, acc):
            off = pl.multiple_of(c * sub, sub)
            ind = jnp.where(pred(sc_ref[pl.ds(off, sub), :], off), 1.0, 0.0)
            for r in range(sub // 8):
                acc = acc + ind[8 * r:8 * (r + 1)]
            return acc
        acc = lax.fori_loop(0, n_sub, body, jnp.zeros((8, TILE), F32))
        return jnp.sum(acc, axis=0, keepdims=True)

    def mid_of(lo, hi):
        return (lo >> 1) + (hi >> 1) + (lo & hi & 1)

    kf = float(topk)
    qpos = i * TILE + lax.broadcasted_iota(jnp.int32, (1, TILE), 1)
    n_adm = ((qpos // CHUNK + 1) * CHUNK).astype(F32)

    def b_cond(c):
        it, lo, hi, _ = c
        return jnp.logical_and(it < 40, _any(mid_of(lo, hi) != lo))

    def b_body(c):
        it, lo, hi, cnt = c
        mid = mid_of(lo, hi)
        active = mid != lo
        thr = _key_to_float(mid)
        cm = count(lambda blk, off: blk >= thr)
        ge = cm >= kf
        take_lo = jnp.logical_and(active, ge)
        new_hi = jnp.where(jnp.logical_and(active, jnp.logical_not(ge)), mid, hi)
        new_hi = jnp.where(jnp.logical_and(active, cm == kf), mid + 1, new_hi)
        return it + 1, jnp.where(take_lo, mid, lo), new_hi, jnp.where(take_lo, cm, cnt)

    init = (jnp.int32(0), jnp.full((1, TILE), KEY_LO, jnp.int32),
            jnp.full((1, TILE), KEY_HI, jnp.int32), n_adm)
    _, lo, _, cnt = lax.while_loop(b_cond, b_body, init)
    thr = _key_to_float(lo)

    excess = cnt > kf

    @pl.when(_any(excess))
    def _():
        need = kf - count(lambda blk, off: blk > thr)

        def tie_le(jmax):
            def pred(blk, off):
                idx = off + lax.broadcasted_iota(jnp.int32, (sub, TILE), 0)
                return jnp.logical_and(blk == thr, idx <= jmax)
            return count(pred)

        def j_body(_, c):
            jlo, jhi = c
            jmid = (jlo + jhi) >> 1
            ok = tie_le(jmid) >= need
            return jnp.where(ok, jlo, jmid), jnp.where(ok, jmid, jhi)

        n_keys = (i + 1) * TILE
        j0 = (jnp.full((1, TILE), -1, jnp.int32), jnp.zeros((1, TILE), jnp.int32) + (n_keys - 1))
        _, jcut = lax.fori_loop(0, 14, j_body, j0)

        def drop(c, carry):
            off = pl.multiple_of(c * sub, sub)
            blk = sc_ref[pl.ds(off, sub), :]
            idx = off + lax.broadcasted_iota(jnp.int32, (sub, TILE), 0)
            kill = jnp.logical_and(jnp.logical_and(blk == thr, idx > jcut), excess)
            sc_ref[pl.ds(off, sub), :] = jnp.where(kill, NEG_INF, blk)
            return carry

        lax.fori_loop(0, n_sub, drop, 0)

    m_ref[...] = jnp.full((DSA_HEADS, TILE), M_INIT, F32)
    l_ref[...] = jnp.zeros((DSA_HEADS, TILE), F32)
    acc_ref[...] = jnp.zeros((DSA_WIDTH, TILE), F32)

    def attend(j, near):
        off = pl.multiple_of(j * TILE, TILE)
        sel = sc_ref[pl.ds(off, TILE), :] >= thr
        for h in range(DSA_HEADS):
            g = h // 2
            s = _nn(kb_ref[0, pl.ds(off, TILE), LANES * g:LANES * (g + 1)], qm_ref[h])
            if near is not None:
                s = s + bias_ref[h, near]
            s = jnp.where(sel, s, NEG_INF)
            m_old = m_ref[h:h + 1, :]
            m_new = jnp.maximum(m_old, jnp.max(s, axis=0, keepdims=True))
            p = jnp.exp(s - m_new)
            alpha = jnp.exp(m_old - m_new)
            l_ref[h:h + 1, :] = alpha * l_ref[h:h + 1, :] + jnp.sum(p, axis=0, keepdims=True)
            m_ref[h:h + 1, :] = m_new
            pv = _nn(vbt_ref[0, j, LANES * g:LANES * (g + 1), :], p.astype(MXU_DTYPE))
            r0 = DSA_HEAD_DIM * (h % 2)
            hs = slice(DSA_HEAD_DIM * h, DSA_HEAD_DIM * (h + 1))
            acc_ref[hs, :] = alpha * acc_ref[hs, :] + pv[r0:r0 + DSA_HEAD_DIM]

    def far_body(j, c):
        attend(j, None)
        return c

    lax.fori_loop(0, i - 1, far_body, 0)

    @pl.when(i >= 1)
    def _():
        attend(i - 1, 0)

    attend(i, 1)

    for h in range(DSA_HEADS):
        hs = slice(DSA_HEAD_DIM * h, DSA_HEAD_DIM * (h + 1))
        acc_ref[hs, :] = acc_ref[hs, :] / l_ref[h:h + 1, :]
    o_ref[0] = (acc_ref[...].T * zb_ref[0].astype(F32)).astype(MXU_DTYPE)


def _merge_kernel(x_ref, ya_ref, yb_ref, ga_ref, gb_ref, woa_ref, wob_ref, wout_ref, fg_ref, o_ref):
    merged = (ga_ref[0].astype(F32) * _nn(ya_ref[0], woa_ref[...])
              + gb_ref[0].astype(F32) * _nn(yb_ref[0], wob_ref[...]))
    y = x_ref[0] + _nn(merged.astype(MXU_DTYPE), wout_ref[...])
    o_ref[0] = _rms(y, fg_ref[...])


def _t5_bucket(rel):
    nb = REL_BUCKETS // 2
    max_exact = nb // 2
    ret = (rel > 0).astype(jnp.int32) * nb
    n = jnp.abs(rel)
    nf = jnp.maximum(n, 1).astype(jnp.float32)
    large = max_exact + (jnp.log(nf / max_exact) / np.log(REL_MAX_DIST / max_exact)
                         * (nb - max_exact)).astype(jnp.int32)
    large = jnp.minimum(large, nb - 1)
    return ret + jnp.where(n < max_exact, n, large)


def _bias_tables(rel_bias):
    key = jnp.arange(TILE, dtype=jnp.int32)[:, None]
    qry = jnp.arange(TILE, dtype=jnp.int32)[None, :]
    rel_diag = key - qry
    far = rel_bias[_t5_bucket(jnp.int32(-(TILE + 1)))]

    def table(rel):
        return jnp.transpose(rel_bias[_t5_bucket(rel)] - far, (2, 0, 1))

    diag = jnp.where(((key // CHUNK) <= (qry // CHUNK))[None], table(rel_diag), NEG_INF)
    return jnp.stack([table(rel_diag - TILE), diag], axis=1).astype(F32)


def _rope_tables(seq):
    half = MLA_ROPE // 2
    freqs = ROPE_BASE ** (-jnp.arange(half, dtype=jnp.float32) / half)
    ang = jnp.arange(seq, dtype=jnp.int32).astype(jnp.float32)[:, None] * freqs[None, :]
    cos, sin = jnp.cos(ang), jnp.sin(ang)
    zeros = jnp.zeros((seq, LANES - MLA_NOPE - MLA_ROPE), F32)
    lead = jnp.zeros((seq, MLA_NOPE), F32)
    c_k = jnp.concatenate([lead, cos, cos, zeros], axis=1)
    s_k = jnp.concatenate([lead, -sin, sin, zeros], axis=1)
    scale = (MLA_NOPE + MLA_ROPE) ** -0.5
    c_q = jnp.concatenate([lead + 1.0, cos, cos, zeros], axis=1) * scale
    return c_q.T, (s_k * scale).T, c_k, s_k


def _prep_weights(w_in, w_uq, w_ukv):
    cuts = np.cumsum(IN_SPLITS)[:-1].tolist()
    (w_qlat, w_ckv, w_kr, w_za, w_qb, w_kb, w_vb, w_zb,
     w_qi, w_ki, w_wi, w_ga, w_gb) = jnp.split(w_in, cuts, axis=1)
    half = MLA_ROPE // 2
    swap = np.concatenate([np.arange(half, MLA_ROPE), np.arange(half)])

    def z(n):
        return jnp.zeros((D_MODEL, n), F32)

    pad = LANES - MLA_NOPE - MLA_ROPE
    kr = jnp.concatenate([z(MLA_NOPE), w_kr, z(pad)], axis=1)
    krs = jnp.concatenate([z(MLA_NOPE), w_kr[:, swap], z(pad)], axis=1)
    ki4 = jnp.tile(w_ki, (1, LANES // IDX_DIM))
    w1 = jnp.concatenate([w_qlat, w_ckv, kr, krs, w_za, w_kb, w_zb, ki4, w_ga, w_gb], axis=1)
    w2t = jnp.concatenate([w_qb, w_vb, w_qi, w_wi, z(16 - IDX_HEADS)], axis=1).T

    wuq = w_uq.reshape(MLA_Q_RANK, MLA_HEADS, MLA_NOPE + MLA_ROPE)
    zq = jnp.zeros((MLA_Q_RANK, MLA_HEADS, pad), F32)
    wuqm = jnp.concatenate([wuq, zq], axis=2).reshape(MLA_Q_RANK, MLA_HEADS * LANES)
    wuqs = jnp.concatenate([jnp.zeros((MLA_Q_RANK, MLA_HEADS, MLA_NOPE), F32),
                            wuq[:, :, MLA_NOPE:][:, :, swap], zq], axis=2
                           ).reshape(MLA_Q_RANK, MLA_HEADS * LANES)
    wukv = w_ukv.reshape(MLA_KV_RANK, MLA_HEADS, MLA_NOPE + MLA_V)
    wk = jnp.concatenate([wukv[:, :, :MLA_NOPE],
                          jnp.zeros((MLA_KV_RANK, MLA_HEADS, LANES - MLA_NOPE), F32)], axis=2
                         ).reshape(MLA_KV_RANK, MLA_HEADS * LANES)
    wv = wukv[:, :, MLA_NOPE:].reshape(MLA_KV_RANK, MLA_WIDTH)
    c = lambda a: a.astype(MXU_DTYPE)
    return c(w1), c(w2t), c(wuqm.T), c(wuqs.T), c(wk), c(wv.T)


def _full(shape):
    return pl.BlockSpec(shape, lambda *_: (0,) * len(shape))


def _resident(shape, index_map):
    return pl.BlockSpec(shape, index_map, pipeline_mode=pl.Buffered(1))


def kernel(x, norm_g, w_in, g_q_lat, w_uq, g_kv_lat, w_ukv, w_o_a, w_o_b, w_out, rel_bias, final_g):
    B, S, D = x.shape
    assert D == D_MODEL and S % TILE == 0 and norm_g.shape[0] == 1
    nb = S // TILE
    topk = min(TOPK_MAX, S // 4)
    bf = MXU_DTYPE
    params = functools.partial(pltpu.CompilerParams, vmem_limit_bytes=VMEM_LIMIT)

    w1, w2t, wuqm_t, wuqs_t, wk, wv_t = _prep_weights(w_in[0], w_uq[0], w_ukv[0])
    cq_t, sq_t, c_k, s_k = _rope_tables(S)
    n1, n2 = w1.shape[1], w2t.shape[0]

    tok = lambda c: pl.BlockSpec((1, TILE, c), lambda b, r: (b, r, 0))
    tok_t = lambda c: pl.BlockSpec((1, c, TILE), lambda b, r: (b, 0, r))
    blk_t = lambda c: pl.BlockSpec((1, 1, c, TILE), lambda b, r: (b, r, 0, 0))
    sds = jax.ShapeDtypeStruct
    (qa_t, ka, va_t, za, qb_t, kb, vb_t, zb, qi_t, ki, wi_t, ga, gb) = pl.pallas_call(
        _proj_kernel,
        grid=(B, nb),
        in_specs=[tok(D), _full((1, D)), _full((D, n1)), _full((n2, D)),
                  _full((1, MLA_Q_RANK)), _full((MLA_HEADS * LANES, MLA_Q_RANK)),
                  _full((MLA_HEADS * LANES, MLA_Q_RANK)),
                  _full((1, MLA_KV_RANK)), _full((MLA_KV_RANK, MLA_HEADS * LANES)),
                  _full((MLA_WIDTH, MLA_KV_RANK)),
                  pl.BlockSpec((LANES, TILE), lambda b, r: (0, r)),
                  pl.BlockSpec((LANES, TILE), lambda b, r: (0, r)),
                  pl.BlockSpec((TILE, LANES), lambda b, r: (r, 0)),
                  pl.BlockSpec((TILE, LANES), lambda b, r: (r, 0))],
        out_specs=[tok_t(MLA_HEADS * LANES), tok(MLA_HEADS * LANES), blk_t(MLA_WIDTH), tok(MLA_WIDTH),
                   tok_t(DSA_WIDTH), tok(DSA_WIDTH), blk_t(DSA_WIDTH), tok(DSA_WIDTH),
                   tok_t(IDX_HEADS * IDX_DIM), tok(LANES), tok_t(IDX_HEADS), tok(D), tok(D)],
        out_shape=[sds((B, MLA_HEADS * LANES, S), bf), sds((B, S, MLA_HEADS * LANES), bf),
                   sds((B, nb, MLA_WIDTH, TILE), bf), sds((B, S, MLA_WIDTH), bf),
                   sds((B, DSA_WIDTH, S), bf), sds((B, S, DSA_WIDTH), bf),
                   sds((B, nb, DSA_WIDTH, TILE), bf), sds((B, S, DSA_WIDTH), bf),
                   sds((B, IDX_HEADS * IDX_DIM, S), bf), sds((B, S, LANES), bf),
                   sds((B, IDX_HEADS, S), F32), sds((B, S, D), bf), sds((B, S, D), bf)],
        compiler_params=params(dimension_semantics=("parallel", "parallel")),
        name="proj",
    )(x, norm_g, w1, w2t, g_q_lat, wuqm_t, wuqs_t, g_kv_lat, wk, wv_t, cq_t, sq_t, c_k, s_k)

    ya = pl.pallas_call(
        _mla_kernel,
        grid=(B, MLA_HEADS // 2, nb),
        in_specs=[pl.BlockSpec((1, 2 * LANES, TILE), lambda b, g, i: (b, g, i)),
                  pl.BlockSpec((1, S, 2 * LANES), lambda b, g, i: (b, 0, g)),
                  pl.BlockSpec((1, nb, LANES, TILE), lambda b, g, i: (b, 0, g, 0)),
                  pl.BlockSpec((1, TILE, LANES), lambda b, g, i: (b, i, g))],
        out_specs=pl.BlockSpec((1, TILE, LANES), lambda b, g, i: (b, i, g)),
        out_shape=sds((B, S, MLA_WIDTH), bf),
        compiler_params=params(dimension_semantics=("parallel", "parallel", "arbitrary")),
        name="mla",
    )(qa_t, ka, va_t, za)

    yb = pl.pallas_call(
        functools.partial(_dsa_kernel, topk),
        grid=(B, nb),
        in_specs=[pl.BlockSpec((1, DSA_WIDTH, TILE), lambda b, i: (b, 0, i)),
                  _resident((1, S, DSA_WIDTH), lambda b, i: (b, 0, 0)),
                  _resident((1, nb, DSA_WIDTH, TILE), lambda b, i: (b, 0, 0, 0)),
                  pl.BlockSpec((1, IDX_HEADS * IDX_DIM, TILE), lambda b, i: (b, 0, i)),
                  _resident((1, S, LANES), lambda b, i: (b, 0, 0)),
                  pl.BlockSpec((1, IDX_HEADS, TILE), lambda b, i: (b, 0, i)),
                  pl.BlockSpec((1, TILE, DSA_WIDTH), lambda b, i: (b, i, 0)),
                  _resident((DSA_HEADS, 2, TILE, TILE), lambda b, i: (0, 0, 0, 0))],
        out_specs=pl.BlockSpec((1, TILE, DSA_WIDTH), lambda b, i: (b, i, 0)),
        out_shape=sds((B, S, DSA_WIDTH), bf),
        scratch_shapes=[pltpu.VMEM((S, TILE), F32),
                        pltpu.VMEM((DSA_HEADS, LANES, TILE), bf),
                        pltpu.VMEM((IDX_HEADS, LANES, TILE), bf),
                        pltpu.VMEM((DSA_HEADS, TILE), F32),
                        pltpu.VMEM((DSA_HEADS, TILE), F32),
                        pltpu.VMEM((DSA_WIDTH, TILE), F32)],
        compiler_params=params(dimension_semantics=("parallel", "arbitrary")),
        name="dsa",
    )(qb_t, kb, vb_t, qi_t, ki, wi_t, zb, _bias_tables(rel_bias))

    return pl.pallas_call(
        _merge_kernel,
        grid=(B, nb),
        in_specs=[tok(D), tok(MLA_WIDTH), tok(DSA_WIDTH), tok(D), tok(D),
                  _full((MLA_WIDTH, D)), _full((DSA_WIDTH, D)), _full((D, D)), _full((1, D))],
        out_specs=tok(D),
        out_shape=sds((B, S, D), x.dtype),
        compiler_params=params(dimension_semantics=("parallel", "parallel")),
        name="merge",
    )(x, ya, yb, ga, gb, w_o_a[0].astype(bf), w_o_b[0].astype(bf), w_out[0].astype(bf),
      final_g.reshape(1, D))
```

```python
import functools

import numpy as np
import jax
import jax.numpy as jnp
from jax import lax
from jax.experimental import pallas as pl
from jax.experimental.pallas import tpu as pltpu

D_MODEL = 1024
CHUNK = 64
EPS = 1e-6
MLA_HEADS = 8
MLA_NOPE = 64
MLA_ROPE = 32
MLA_V = 64
MLA_Q_RANK = 384
MLA_KV_RANK = 256
ROPE_BASE = 10000.0
MLA_WIDTH = MLA_HEADS * MLA_V
DSA_HEADS = 8
DSA_HEAD_DIM = 64
DSA_WIDTH = DSA_HEADS * DSA_HEAD_DIM
IDX_HEADS = 8
IDX_DIM = 32
TOPK_MAX = 256
REL_BUCKETS = 32
REL_MAX_DIST = 128
IN_SPLITS = (MLA_Q_RANK, MLA_KV_RANK, MLA_ROPE, MLA_WIDTH,
             DSA_WIDTH, DSA_WIDTH, DSA_WIDTH, DSA_WIDTH,
             IDX_HEADS * IDX_DIM, IDX_DIM, IDX_HEADS,
             D_MODEL, D_MODEL)

LANES = 128
TILE = 256
ROW_CHUNK = 64
VMEM_LIMIT = 56 * 1024 * 1024
MXU_DTYPE = jnp.bfloat16
F32 = jnp.float32
NEG_INF = float("-inf")
M_INIT = -1e30

KEY_LO = int(np.int32(np.uint32(0x80800000)))
KEY_HI = 0x7F800001


def _nt(a, b):
    return lax.dot_general(a, b, (((1,), (1,)), ((), ())), preferred_element_type=F32)


def _nn(a, b):
    return jnp.dot(a, b, preferred_element_type=F32)


def _sigmoid(v):
    return 1.0 / (1.0 + jnp.exp(-v))


def _rms(v, g):
    return v * lax.rsqrt(jnp.mean(v * v, axis=-1, keepdims=True) + EPS) * g


def _proj_kernel(x_ref, ng_ref, w1_ref, w2t_ref, gq_ref, wuqm_ref, wuqs_ref, gkv_ref, wk_ref,
                 wvt_ref, cqt_ref, sqt_ref, ck_ref, sk_ref,
                 qat_ref, ka_ref, vat_ref, za_ref, qbt_ref, kb_ref, vbt_ref, zb_ref,
                 qit_ref, ki_ref, wit_ref, ga_ref, gb_ref):
    hb = _rms(x_ref[0], ng_ref[...]).astype(MXU_DTYPE)

    def cols(lo, hi):
        return _nn(hb, w1_ref[:, lo:hi])

    def rows(lo, hi):
        return _nt(w2t_ref[lo:hi, :], hb)

    qn = _rms(cols(0, 384), gq_ref[...]).astype(MXU_DTYPE)
    qm = _nt(wuqm_ref[...], qn)
    qs = _nt(wuqs_ref[...], qn)
    cqt = cqt_ref[...]
    sqt = sqt_ref[...]
    for h in range(MLA_HEADS):
        sl = slice(LANES * h, LANES * (h + 1))
        qat_ref[0, sl, :] = (qm[sl] * cqt + qs[sl] * sqt).astype(MXU_DTYPE)

    kvn = _rms(cols(384, 640), gkv_ref[...]).astype(MXU_DTYPE)
    kn = _nn(kvn, wk_ref[...])
    kpe = cols(640, 768) * ck_ref[...] + cols(768, 896) * sk_ref[...]
    for h in range(MLA_HEADS):
        sl = slice(LANES * h, LANES * (h + 1))
        ka_ref[0, :, sl] = (kn[:, sl] + kpe).astype(MXU_DTYPE)
    vat_ref[0, 0] = _nt(wvt_ref[...], kvn).astype(MXU_DTYPE)

    za = cols(896, 1408)
    za_ref[0] = (za * _sigmoid(za)).astype(MXU_DTYPE)
    kb_ref[0] = cols(1408, 1920).astype(MXU_DTYPE)
    zb = cols(1920, 2432)
    zb_ref[0] = (zb * _sigmoid(zb)).astype(MXU_DTYPE)
    ki_ref[0] = cols(2432, 2560).astype(MXU_DTYPE)
    ga_ref[0] = _sigmoid(cols(2560, 3584)).astype(MXU_DTYPE)
    gb_ref[0] = _sigmoid(cols(3584, 4608)).astype(MXU_DTYPE)

    qbt_ref[0] = (rows(0, 512) * (DSA_HEAD_DIM ** -0.5)).astype(MXU_DTYPE)
    vbt_ref[0, 0] = rows(512, 1024).astype(MXU_DTYPE)
    qit_ref[0] = rows(1024, 1280).astype(MXU_DTYPE)
    wit_ref[0] = rows(1280, 1296)[:IDX_HEADS] * ((IDX_DIM * IDX_HEADS) ** -0.5)


def _diag_admissible():
    key = lax.broadcasted_iota(jnp.int32, (TILE, TILE), 0)
    qry = lax.broadcasted_iota(jnp.int32, (TILE, TILE), 1)
    return (key // CHUNK) <= (qry // CHUNK)


def _flash_init(m_ref, l_ref, acc_ref):
    m_ref[...] = jnp.full(m_ref.shape, M_INIT, F32)
    l_ref[...] = jnp.zeros(l_ref.shape, F32)
    acc_ref[...] = jnp.zeros(acc_ref.shape, F32)


def _flash_block(score_fns, value_fns, m_ref, l_ref, acc_ref, s_ref, p_ref):
    n = len(score_fns)
    rows = ROW_CHUNK
    n_chunks = TILE // rows
    for h in range(n):
        s_ref[h] = score_fns[h]()
    alphas = []
    for h in range(n):
        m_old = m_ref[h]
        mx = s_ref[h, 0:rows]
        for c in range(1, n_chunks):
            mx = jnp.maximum(mx, s_ref[h, rows * c:rows * (c + 1)])
        m_new = jnp.maximum(m_old, jnp.max(mx, axis=0, keepdims=True))
        psum = jnp.zeros((rows, TILE), F32)
        for c in range(n_chunks):
            p = jnp.exp(s_ref[h, rows * c:rows * (c + 1)] - m_new)
            psum = psum + p
            p_ref[h, rows * c:rows * (c + 1)] = p.astype(MXU_DTYPE)
        alpha = jnp.exp(m_old - m_new)
        l_ref[h] = alpha * l_ref[h] + jnp.sum(psum, axis=0, keepdims=True)
        m_ref[h] = m_new
        alphas.append(alpha)
    for h in range(n):
        acc_ref[h] = alphas[h] * acc_ref[h] + _nn(value_fns[h](), p_ref[h])


def _flash_finish(n_heads, z_ref, o_ref, l_ref, acc_ref):
    for h in range(n_heads):
        acc_ref[h] = acc_ref[h] / l_ref[h]
    y_t = acc_ref[...].reshape(n_heads * acc_ref.shape[1], TILE)
    o_ref[0] = (y_t.T * z_ref[0].astype(F32)).astype(MXU_DTYPE)


def _mla_kernel(qt_ref, k_ref, vt_ref, z_ref, o_ref, m_ref, l_ref, acc_ref, s_ref, p_ref):
    i = pl.program_id(1)
    _flash_init(m_ref, l_ref, acc_ref)

    def attend(j, masked):
        off = pl.multiple_of(j * TILE, TILE)

        def score(h):
            sl = slice(LANES * h, LANES * (h + 1))
            s = _nn(k_ref[0, pl.ds(off, TILE), sl], qt_ref[0, sl, :])
            return jnp.where(_diag_admissible(), s, NEG_INF) if masked else s

        def value(h):
            return vt_ref[0, j, MLA_V * h:MLA_V * (h + 1), :]

        heads = range(MLA_HEADS)
        _flash_block([functools.partial(score, h) for h in heads],
                     [functools.partial(value, h) for h in heads],
                     m_ref, l_ref, acc_ref, s_ref, p_ref)

    def body(j, c):
        attend(j, False)
        return c

    lax.fori_loop(0, i, body, 0)
    attend(i, True)
    _flash_finish(MLA_HEADS, z_ref, o_ref, l_ref, acc_ref)


def _key_to_float(k):
    bits = k ^ ((k >> 31) & 0x7FFFFFFF)
    return lax.bitcast_convert_type(bits, F32)


def _any(mask):
    return jnp.max(jnp.where(mask, 1.0, 0.0)) > 0.5


def _dsa_kernel(topk, qbt_ref, kb_ref, vbt_ref, qit_ref, ki_ref, wit_ref, zb_ref, bias_ref, o_ref,
                sc_ref, qm_ref, qim_ref, sel_ref, m_ref, l_ref, acc_ref, s_ref, p_ref):
    i = pl.program_id(1)
    adm = _diag_admissible()
    sub = 64
    n_sub = (i + 1) * (TILE // sub)

    row = lax.broadcasted_iota(jnp.int32, (LANES, TILE), 0)
    for h in range(DSA_HEADS):
        g = h // 2
        qm_ref[h] = jnp.where((row // DSA_HEAD_DIM) == (h % 2),
                              qbt_ref[0, LANES * g:LANES * (g + 1), :], 0).astype(MXU_DTYPE)
        gi = h // 4
        qim_ref[h] = jnp.where((row // IDX_DIM) == (h % 4),
                               qit_ref[0, LANES * gi:LANES * (gi + 1), :], 0).astype(MXU_DTYPE)

    def score_block(j, masked):
        off = pl.multiple_of(j * TILE, TILE)
        kij = ki_ref[0, pl.ds(off, TILE), :]
        tot = jnp.zeros((TILE, TILE), F32)
        for h in range(IDX_HEADS):
            s = _nn(kij, qim_ref[h])
            tot = tot + jnp.maximum(s, 0.0) * wit_ref[0, h:h + 1, :]
        if masked:
            tot = jnp.where(adm, tot, NEG_INF)
        sc_ref[pl.ds(off, TILE), :] = tot

    def p1(j, c):
        score_block(j, False)
        return c

    lax.fori_loop(0, i, p1, 0)
    score_block(i, True)

    def count(pred):
        def body(c, acc):
            off = pl.multiple_of(c * sub, sub)
            ind = jnp.where(pred(sc_ref[pl.ds(off, sub), :], off), 1.0, 0.0)
            for r in range(sub // 8):
                acc = acc + ind[8 * r:8 * (r + 1)]
            return acc
        acc = lax.fori_loop(0, n_sub, body, jnp.zeros((8, TILE), F32))
        return jnp.sum(acc, axis=0, keepdims=True)

    def mid_of(lo, hi):
        return (lo >> 1) + (hi >> 1) + (lo & hi & 1)

    kf = float(topk)
    qpos = i * TILE + lax.broadcasted_iota(jnp.int32, (1, TILE), 1)
    n_adm = ((qpos // CHUNK + 1) * CHUNK).astype(F32)

    def b_cond(c):
        it, lo, hi, _ = c
        return jnp.logical_and(it < 40, _any(mid_of(lo, hi) != lo))

    def b_body(c):
        it, lo, hi, cnt = c
        mid = mid_of(lo, hi)
        active = mid != lo
        thr = _key_to_float(mid)
        cm = count(lambda blk, off: blk >= thr)
        ge = cm >= kf
        take_lo = jnp.logical_and(active, ge)
        new_hi = jnp.where(jnp.logical_and(active, jnp.logical_not(ge)), mid, hi)
        new_hi = jnp.where(jnp.logical_and(active, cm == kf), mid + 1, new_hi)
        return it + 1, jnp.where(take_lo, mid, lo), new_hi, jnp.where(take_lo, cm, cnt)

    init = (jnp.int32(0), jnp.full((1, TILE), KEY_LO, jnp.int32),
            jnp.full((1, TILE), KEY_HI, jnp.int32), n_adm)
    _, lo, _, cnt = lax.while_loop(b_cond, b_body, init)
    thr = _key_to_float(lo)

    excess = cnt > kf

    @pl.when(_any(excess))
    def _():
        need = kf - count(lambda blk, off: blk > thr)

        def tie_le(jmax):
            def pred(blk, off):
                idx = off + lax.broadcasted_iota(jnp.int32, (sub, TILE), 0)
                return jnp.logical_and(blk == thr, idx <= jmax)
            return count(pred)

        def j_body(_, c):
            jlo, jhi = c
            jmid = (jlo + jhi) >> 1
            ok = tie_le(jmid) >= need
            return jnp.where(ok, jlo, jmid), jnp.where(ok, jmid, jhi)

        n_keys = (i + 1) * TILE
        j0 = (jnp.full((1, TILE), -1, jnp.int32), jnp.zeros((1, TILE), jnp.int32) + (n_keys - 1))
        _, jcut = lax.fori_loop(0, 14, j_body, j0)

        def drop(c, carry):
            off = pl.multiple_of(c * sub, sub)
            blk = sc_ref[pl.ds(off, sub), :]
            idx = off + lax.broadcasted_iota(jnp.int32, (sub, TILE), 0)
            kill = jnp.logical_and(jnp.logical_and(blk == thr, idx > jcut), excess)
            sc_ref[pl.ds(off, sub), :] = jnp.where(kill, NEG_INF, blk)
            return carry

        lax.fori_loop(0, n_sub, drop, 0)

    _flash_init(m_ref, l_ref, acc_ref)

    def attend(j, near):
        off = pl.multiple_of(j * TILE, TILE)
        sel_ref[...] = jnp.where(sc_ref[pl.ds(off, TILE), :] >= thr, 0.0, NEG_INF)

        def score(h):
            g = h // 2
            s = _nn(kb_ref[0, pl.ds(off, TILE), LANES * g:LANES * (g + 1)], qm_ref[h])
            if near is not None:
                s = s + bias_ref[h, near]
            return s + sel_ref[...]

        def value(h):
            return vbt_ref[0, j, DSA_HEAD_DIM * h:DSA_HEAD_DIM * (h + 1), :]

        heads = range(DSA_HEADS)
        _flash_block([functools.partial(score, h) for h in heads],
                     [functools.partial(value, h) for h in heads],
                     m_ref, l_ref, acc_ref, s_ref, p_ref)

    def far_body(j, c):
        attend(j, None)
        return c

    lax.fori_loop(0, i - 1, far_body, 0)

    @pl.when(i >= 1)
    def _():
        attend(i - 1, 0)

    attend(i, 1)

    _flash_finish(DSA_HEADS, zb_ref, o_ref, l_ref, acc_ref)


def _merge_kernel(x_ref, ya_ref, yb_ref, ga_ref, gb_ref, woa_ref, wob_ref, wout_ref, fg_ref, o_ref):
    merged = (ga_ref[0].astype(F32) * _nn(ya_ref[0], woa_ref[...])
              + gb_ref[0].astype(F32) * _nn(yb_ref[0], wob_ref[...]))
    y = x_ref[0] + _nn(merged.astype(MXU_DTYPE), wout_ref[...])
    o_ref[0] = _rms(y, fg_ref[...])


def _t5_bucket(rel):
    nb = REL_BUCKETS // 2
    max_exact = nb // 2
    ret = (rel > 0).astype(jnp.int32) * nb
    n = jnp.abs(rel)
    nf = jnp.maximum(n, 1).astype(jnp.float32)
    large = max_exact + (jnp.log(nf / max_exact) / np.log(REL_MAX_DIST / max_exact)
                         * (nb - max_exact)).astype(jnp.int32)
    large = jnp.minimum(large, nb - 1)
    return ret + jnp.where(n < max_exact, n, large)


def _bias_tables(rel_bias):
    key = jnp.arange(TILE, dtype=jnp.int32)[:, None]
    qry = jnp.arange(TILE, dtype=jnp.int32)[None, :]
    rel_diag = key - qry
    far = rel_bias[_t5_bucket(jnp.int32(-(TILE + 1)))]

    def table(rel):
        bucket = _t5_bucket(rel)[None]
        out = jnp.zeros((DSA_HEADS, TILE, TILE), F32)
        for b in range(REL_BUCKETS):
            out = jnp.where(bucket == b, rel_bias[b][:, None, None], out)
        return out - far[:, None, None]

    diag = jnp.where(((key // CHUNK) <= (qry // CHUNK))[None], table(rel_diag), NEG_INF)
    return jnp.stack([table(rel_diag - TILE), diag], axis=1).astype(F32)


def _rope_tables(seq):
    half = MLA_ROPE // 2
    freqs = ROPE_BASE ** (-jnp.arange(half, dtype=jnp.float32) / half)
    ang = jnp.arange(seq, dtype=jnp.int32).astype(jnp.float32)[:, None] * freqs[None, :]
    cos, sin = jnp.cos(ang), jnp.sin(ang)
    zeros = jnp.zeros((seq, LANES - MLA_NOPE - MLA_ROPE), F32)
    lead = jnp.zeros((seq, MLA_NOPE), F32)
    c_k = jnp.concatenate([lead, cos, cos, zeros], axis=1)
    s_k = jnp.concatenate([lead, -sin, sin, zeros], axis=1)
    scale = (MLA_NOPE + MLA_ROPE) ** -0.5
    c_q = jnp.concatenate([lead + 1.0, cos, cos, zeros], axis=1) * scale
    return c_q.T, (s_k * scale).T, c_k, s_k


def _prep_weights(w_in, w_uq, w_ukv):
    cuts = np.cumsum(IN_SPLITS)[:-1].tolist()
    (w_qlat, w_ckv, w_kr, w_za, w_qb, w_kb, w_vb, w_zb,
     w_qi, w_ki, w_wi, w_ga, w_gb) = jnp.split(w_in, cuts, axis=1)
    half = MLA_ROPE // 2
    swap = np.concatenate([np.arange(half, MLA_ROPE), np.arange(half)])

    def z(n):
        return jnp.zeros((D_MODEL, n), F32)

    pad = LANES - MLA_NOPE - MLA_ROPE
    kr = jnp.concatenate([z(MLA_NOPE), w_kr, z(pad)], axis=1)
    krs = jnp.concatenate([z(MLA_NOPE), w_kr[:, swap], z(pad)], axis=1)
    ki4 = jnp.tile(w_ki, (1, LANES // IDX_DIM))
    w1 = jnp.concatenate([w_qlat, w_ckv, kr, krs, w_za, w_kb, w_zb, ki4, w_ga, w_gb], axis=1)
    w2t = jnp.concatenate([w_qb, w_vb, w_qi, w_wi, z(16 - IDX_HEADS)], axis=1).T

    wuq = w_uq.reshape(MLA_Q_RANK, MLA_HEADS, MLA_NOPE + MLA_ROPE)
    zq = jnp.zeros((MLA_Q_RANK, MLA_HEADS, pad), F32)
    wuqm = jnp.concatenate([wuq, zq], axis=2).reshape(MLA_Q_RANK, MLA_HEADS * LANES)
    wuqs = jnp.concatenate([jnp.zeros((MLA_Q_RANK, MLA_HEADS, MLA_NOPE), F32),
                            wuq[:, :, MLA_NOPE:][:, :, swap], zq], axis=2
                           ).reshape(MLA_Q_RANK, MLA_HEADS * LANES)
    wukv = w_ukv.reshape(MLA_KV_RANK, MLA_HEADS, MLA_NOPE + MLA_V)
    wk = jnp.concatenate([wukv[:, :, :MLA_NOPE],
                          jnp.zeros((MLA_KV_RANK, MLA_HEADS, LANES - MLA_NOPE), F32)], axis=2
                         ).reshape(MLA_KV_RANK, MLA_HEADS * LANES)
    wv = wukv[:, :, MLA_NOPE:].reshape(MLA_KV_RANK, MLA_WIDTH)
    c = lambda a: a.astype(MXU_DTYPE)
    return c(w1), c(w2t), c(wuqm.T), c(wuqs.T), c(wk), c(wv.T)


def _full(shape):
    return pl.BlockSpec(shape, lambda *_: (0,) * len(shape))


def _resident(shape, index_map):
    return pl.BlockSpec(shape, index_map, pipeline_mode=pl.Buffered(1))


def kernel(x, norm_g, w_in, g_q_lat, w_uq, g_kv_lat, w_ukv, w_o_a, w_o_b, w_out, rel_bias, final_g):
    B, S, D = x.shape
    assert D == D_MODEL and S % TILE == 0 and norm_g.shape[0] == 1
    nb = S // TILE
    topk = min(TOPK_MAX, S // 4)
    bf = MXU_DTYPE
    params = functools.partial(pltpu.CompilerParams, vmem_limit_bytes=VMEM_LIMIT)

    w1, w2t, wuqm_t, wuqs_t, wk, wv_t = _prep_weights(w_in[0], w_uq[0], w_ukv[0])
    cq_t, sq_t, c_k, s_k = _rope_tables(S)
    n1, n2 = w1.shape[1], w2t.shape[0]

    tok = lambda c: pl.BlockSpec((1, TILE, c), lambda b, r: (b, r, 0))
    tok_t = lambda c: pl.BlockSpec((1, c, TILE), lambda b, r: (b, 0, r))
    blk_t = lambda c: pl.BlockSpec((1, 1, c, TILE), lambda b, r: (b, r, 0, 0))
    sds = jax.ShapeDtypeStruct
    (qa_t, ka, va_t, za, qb_t, kb, vb_t, zb, qi_t, ki, wi_t, ga, gb) = pl.pallas_call(
        _proj_kernel,
        grid=(B, nb),
        in_specs=[tok(D), _full((1, D)), _full((D, n1)), _full((n2, D)),
                  _full((1, MLA_Q_RANK)), _full((MLA_HEADS * LANES, MLA_Q_RANK)),
                  _full((MLA_HEADS * LANES, MLA_Q_RANK)),
                  _full((1, MLA_KV_RANK)), _full((MLA_KV_RANK, MLA_HEADS * LANES)),
                  _full((MLA_WIDTH, MLA_KV_RANK)),
                  pl.BlockSpec((LANES, TILE), lambda b, r: (0, r)),
                  pl.BlockSpec((LANES, TILE), lambda b, r: (0, r)),
                  pl.BlockSpec((TILE, LANES), lambda b, r: (r, 0)),
                  pl.BlockSpec((TILE, LANES), lambda b, r: (r, 0))],
        out_specs=[tok_t(MLA_HEADS * LANES), tok(MLA_HEADS * LANES), blk_t(MLA_WIDTH), tok(MLA_WIDTH),
                   tok_t(DSA_WIDTH), tok(DSA_WIDTH), blk_t(DSA_WIDTH), tok(DSA_WIDTH),
                   tok_t(IDX_HEADS * IDX_DIM), tok(LANES), tok_t(IDX_HEADS), tok(D), tok(D)],
        out_shape=[sds((B, MLA_HEADS * LANES, S), bf), sds((B, S, MLA_HEADS * LANES), bf),
                   sds((B, nb, MLA_WIDTH, TILE), bf), sds((B, S, MLA_WIDTH), bf),
                   sds((B, DSA_WIDTH, S), bf), sds((B, S, DSA_WIDTH), bf),
                   sds((B, nb, DSA_WIDTH, TILE), bf), sds((B, S, DSA_WIDTH), bf),
                   sds((B, IDX_HEADS * IDX_DIM, S), bf), sds((B, S, LANES), bf),
                   sds((B, IDX_HEADS, S), F32), sds((B, S, D), bf), sds((B, S, D), bf)],
        compiler_params=params(dimension_semantics=("parallel", "parallel")),
        name="proj",
    )(x, norm_g, w1, w2t, g_q_lat, wuqm_t, wuqs_t, g_kv_lat, wk, wv_t, cq_t, sq_t, c_k, s_k)

    def flash_state(heads, dv):
        return [pltpu.VMEM((heads, 1, TILE), F32), pltpu.VMEM((heads, 1, TILE), F32),
                pltpu.VMEM((heads, dv, TILE), F32),
                pltpu.VMEM((heads, TILE, TILE), F32), pltpu.VMEM((heads, TILE, TILE), bf)]

    ya = pl.pallas_call(
        _mla_kernel,
        grid=(B, nb),
        in_specs=[pl.BlockSpec((1, MLA_HEADS * LANES, TILE), lambda b, i: (b, 0, i)),
                  _resident((1, S, MLA_HEADS * LANES), lambda b, i: (b, 0, 0)),
                  _resident((1, nb, MLA_WIDTH, TILE), lambda b, i: (b, 0, 0, 0)),
                  pl.BlockSpec((1, TILE, MLA_WIDTH), lambda b, i: (b, i, 0))],
        out_specs=pl.BlockSpec((1, TILE, MLA_WIDTH), lambda b, i: (b, i, 0)),
        out_shape=sds((B, S, MLA_WIDTH), bf),
        scratch_shapes=flash_state(MLA_HEADS, MLA_V),
        compiler_params=params(dimension_semantics=("parallel", "arbitrary")),
        name="mla",
    )(qa_t, ka, va_t, za)

    yb = pl.pallas_call(
        functools.partial(_dsa_kernel, topk),
        grid=(B, nb),
        in_specs=[pl.BlockSpec((1, DSA_WIDTH, TILE), lambda b, i: (b, 0, i)),
                  _resident((1, S, DSA_WIDTH), lambda b, i: (b, 0, 0)),
                  _resident((1, nb, DSA_WIDTH, TILE), lambda b, i: (b, 0, 0, 0)),
                  pl.BlockSpec((1, IDX_HEADS * IDX_DIM, TILE), lambda b, i: (b, 0, i)),
                  _resident((1, S, LANES), lambda b, i: (b, 0, 0)),
                  pl.BlockSpec((1, IDX_HEADS, TILE), lambda b, i: (b, 0, i)),
                  pl.BlockSpec((1, TILE, DSA_WIDTH), lambda b, i: (b, i, 0)),
                  _resident((DSA_HEADS, 2, TILE, TILE), lambda b, i: (0, 0, 0, 0))],
        out_specs=pl.BlockSpec((1, TILE, DSA_WIDTH), lambda b, i: (b, i, 0)),
        out_shape=sds((B, S, DSA_WIDTH), bf),
        scratch_shapes=[pltpu.VMEM((S, TILE), F32),
                        pltpu.VMEM((DSA_HEADS, LANES, TILE), bf),
                        pltpu.VMEM((IDX_HEADS, LANES, TILE), bf),
                        pltpu.VMEM((TILE, TILE), F32),
                        *flash_state(DSA_HEADS, DSA_HEAD_DIM)],
        compiler_params=params(dimension_semantics=("parallel", "arbitrary")),
        name="dsa",
    )(qb_t, kb, vb_t, qi_t, ki, wi_t, zb, _bias_tables(rel_bias))

    return pl.pallas_call(
        _merge_kernel,
        grid=(B, nb),
        in_specs=[tok(D), tok(MLA_WIDTH), tok(DSA_WIDTH), tok(D), tok(D),
                  _full((MLA_WIDTH, D)), _full((DSA_WIDTH, D)), _full((D, D)), _full((1, D))],
        out_specs=tok(D),
        out_shape=sds((B, S, D), x.dtype),
        compiler_params=params(dimension_semantics=("parallel", "parallel")),
        name="merge",
    )(x, ya, yb, ga, gb, w_o_a[0].astype(bf), w_o_b[0].astype(bf), w_out[0].astype(bf),
      final_g.reshape(1, D))
```

```python
import functools
import math

import numpy as np
import jax
import jax.numpy as jnp
from jax import lax
from jax.experimental import pallas as pl
from jax.experimental.pallas import tpu as pltpu

D_MODEL = 1024
CHUNK = 64
EPS = 1e-6
MLA_HEADS = 8
MLA_NOPE = 64
MLA_ROPE = 32
MLA_V = 64
MLA_Q_RANK = 384
MLA_KV_RANK = 256
ROPE_BASE = 10000.0
MLA_WIDTH = MLA_HEADS * MLA_V
DSA_HEADS = 8
DSA_HEAD_DIM = 64
DSA_WIDTH = DSA_HEADS * DSA_HEAD_DIM
IDX_HEADS = 8
IDX_DIM = 32
TOPK_MAX = 256
REL_BUCKETS = 32
REL_MAX_DIST = 128
IN_SPLITS = (MLA_Q_RANK, MLA_KV_RANK, MLA_ROPE, MLA_WIDTH,
             DSA_WIDTH, DSA_WIDTH, DSA_WIDTH, DSA_WIDTH,
             IDX_HEADS * IDX_DIM, IDX_DIM, IDX_HEADS,
             D_MODEL, D_MODEL)

LANES = 128
BF16_ROWS = 16
TILE = 256
ROW_CHUNK = 64
HEAD_V = 64
V_ROWS = HEAD_V + BF16_ROWS
VMEM_LIMIT = 56 * 1024 * 1024
MXU_DTYPE = jnp.bfloat16
F32 = jnp.float32
NEG_INF = float("-inf")
M_INIT = -1e30
LOG2E = math.log2(math.e)

KEY_LO = int(np.int32(np.uint32(0x80800000)))


def _nt(a, b):
    return lax.dot_general(a, b, (((1,), (1,)), ((), ())), preferred_element_type=F32)


def _nn(a, b):
    return jnp.dot(a, b, preferred_element_type=F32)


def _sigmoid(v):
    return 1.0 / (1.0 + jnp.exp(-v))


def _rms(v, g):
    return v * lax.rsqrt(jnp.mean(v * v, axis=-1, keepdims=True) + EPS) * g


def _store_values(vt_ref, v_t):
    row = lax.broadcasted_iota(jnp.int32, (BF16_ROWS, TILE), 0)
    ones_row = jnp.where(row == 0, 1.0, 0.0).astype(MXU_DTYPE)
    for h in range(v_t.shape[0] // HEAD_V):
        vt_ref[0, 0, V_ROWS * h:V_ROWS * h + HEAD_V, :] = v_t[HEAD_V * h:HEAD_V * (h + 1)].astype(MXU_DTYPE)
        vt_ref[0, 0, V_ROWS * h + HEAD_V:V_ROWS * (h + 1), :] = ones_row


def _proj_kernel(x_ref, ng_ref, w1_ref, w2t_ref, gq_ref, wuqm_ref, wuqs_ref, gkv_ref, wk_ref,
                 wvt_ref, cqt_ref, sqt_ref, ck_ref, sk_ref,
                 qat_ref, ka_ref, vat_ref, za_ref, qbt_ref, kb_ref, vbt_ref, zb_ref,
                 qit_ref, ki_ref, wit_ref, ga_ref, gb_ref):
    hb = _rms(x_ref[0], ng_ref[...]).astype(MXU_DTYPE)

    def cols(lo, hi):
        return _nn(hb, w1_ref[:, lo:hi])

    def rows(lo, hi):
        return _nt(w2t_ref[lo:hi, :], hb)

    qn = _rms(cols(0, 384), gq_ref[...]).astype(MXU_DTYPE)
    qm = _nt(wuqm_ref[...], qn)
    qs = _nt(wuqs_ref[...], qn)
    cqt = cqt_ref[...]
    sqt = sqt_ref[...]
    for h in range(MLA_HEADS):
        sl = slice(LANES * h, LANES * (h + 1))
        qat_ref[0, sl, :] = (qm[sl] * cqt + qs[sl] * sqt).astype(MXU_DTYPE)

    kvn = _rms(cols(384, 640), gkv_ref[...]).astype(MXU_DTYPE)
    kn = _nn(kvn, wk_ref[...])
    kpe = cols(640, 768) * ck_ref[...] + cols(768, 896) * sk_ref[...]
    for h in range(MLA_HEADS):
        sl = slice(LANES * h, LANES * (h + 1))
        ka_ref[0, :, sl] = (kn[:, sl] + kpe).astype(MXU_DTYPE)
    _store_values(vat_ref, _nt(wvt_ref[...], kvn))

    za = cols(896, 1408)
    za_ref[0] = (za * _sigmoid(za)).astype(MXU_DTYPE)
    kb_ref[0] = cols(1408, 1920).astype(MXU_DTYPE)
    zb = cols(1920, 2432)
    zb_ref[0] = (zb * _sigmoid(zb)).astype(MXU_DTYPE)
    ki_ref[0] = cols(2432, 2560).astype(MXU_DTYPE)
    ga_ref[0] = _sigmoid(cols(2560, 3584)).astype(MXU_DTYPE)
    gb_ref[0] = _sigmoid(cols(3584, 4608)).astype(MXU_DTYPE)

    qbt_ref[0] = (rows(0, 512) * (DSA_HEAD_DIM ** -0.5 * LOG2E)).astype(MXU_DTYPE)
    _store_values(vbt_ref, rows(512, 1024))
    qit_ref[0] = rows(1024, 1280).astype(MXU_DTYPE)
    wit_ref[0] = rows(1280, 1296)[:IDX_HEADS] * ((IDX_DIM * IDX_HEADS) ** -0.5)


def _diag_admissible():
    key = lax.broadcasted_iota(jnp.int32, (TILE, TILE), 0)
    qry = lax.broadcasted_iota(jnp.int32, (TILE, TILE), 1)
    return (key // CHUNK) <= (qry // CHUNK)


def _flash_init(m_ref, acc_ref):
    m_ref[...] = jnp.full(m_ref.shape, M_INIT, F32)
    acc_ref[...] = jnp.zeros(acc_ref.shape, F32)


class _Flash:
    def __init__(self, n, score, value, m_ref, mb_ref, acc_ref, s_refs, prepare=None):
        self.n, self.score, self.value, self.prepare = n, score, value, prepare
        self.m_ref, self.mb_ref, self.acc_ref, self.s_refs = m_ref, mb_ref, acc_ref, s_refs

    def put_scores(self, h, slot, s):
        self.s_refs[h][slot] = s
        mx = s[0:ROW_CHUNK]
        for c in range(1, TILE // ROW_CHUNK):
            mx = jnp.maximum(mx, s[ROW_CHUNK * c:ROW_CHUNK * (c + 1)])
        self.mb_ref[h, slot] = jnp.max(mx, axis=0, keepdims=True)

    def scores(self, j, slot, score=None):
        if self.prepare is not None:
            self.prepare(j, slot)
        for h in range(self.n):
            self.put_scores(h, slot, (score or self.score)(h, j, slot))

    def softmax_values(self, j, slot, nxt=None):
        rows = ROW_CHUNK
        n_chunks = TILE // rows
        if nxt is not None and self.prepare is not None:
            self.prepare(*nxt)
        for h in range(self.n):
            if nxt is not None:
                self.put_scores(h, nxt[1], self.score(h, *nxt))
            s_ref = self.s_refs[h]
            m_old = self.m_ref[h]
            m_new = jnp.maximum(m_old, self.mb_ref[h, slot])
            p = jnp.concatenate(
                [jnp.exp2(s_ref[slot, rows * c:rows * (c + 1)] - m_new).astype(MXU_DTYPE)
                 for c in range(n_chunks)], axis=0)
            self.m_ref[h] = m_new
            alpha = jnp.exp2(m_old - m_new)
            self.acc_ref[h] = alpha * self.acc_ref[h] + _nn(self.value(h, j), p)

    def block(self, j, score):
        self.scores(j, 0, score)
        self.softmax_values(j, 0)

    def pipeline(self, n_blocks):
        @pl.when(n_blocks > 0)
        def _():
            self.scores(0, 0)
            n_pairs = (n_blocks - 1) // 2

            def body(t, c):
                j = 2 * t
                self.softmax_values(j, 0, nxt=(j + 1, 1))
                self.softmax_values(j + 1, 1, nxt=(j + 2, 0))
                return c

            lax.fori_loop(0, n_pairs, body, 0)
            j = 2 * n_pairs

            @pl.when(j + 1 < n_blocks)
            def _():
                self.softmax_values(j, 0, nxt=(j + 1, 1))
                self.softmax_values(j + 1, 1)

            @pl.when(j + 1 == n_blocks)
            def _():
                self.softmax_values(j, 0)


def _flash_finish(n_heads, z_ref, o_ref, acc_ref, y_ref):
    for h in range(n_heads):
        y_ref[HEAD_V * h:HEAD_V * (h + 1), :] = acc_ref[h, 0:HEAD_V] / acc_ref[h, HEAD_V:HEAD_V + 1]
    o_ref[0] = (y_ref[...].T * z_ref[0].astype(F32)).astype(MXU_DTYPE)


def _mla_kernel(qt_ref, k_ref, vt_ref, z_ref, o_ref, m_ref, mb_ref, acc_ref, y_ref, *s_refs):
    i = pl.program_id(1)
    _flash_init(m_ref, acc_ref)

    def score(h, j, slot):
        sl = slice(LANES * h, LANES * (h + 1))
        off = pl.multiple_of(j * TILE, TILE)
        return _nn(k_ref[0, pl.ds(off, TILE), sl], qt_ref[0, sl, :])

    def diag_score(h, j, slot):
        return jnp.where(_diag_admissible(), score(h, j, slot), NEG_INF)

    def value(h, j):
        return vt_ref[0, j, V_ROWS * h:V_ROWS * (h + 1), :]

    flash = _Flash(MLA_HEADS, score, value, m_ref, mb_ref, acc_ref, s_refs)
    flash.block(i, diag_score)
    flash.pipeline(i)
    _flash_finish(MLA_HEADS, z_ref, o_ref, acc_ref, y_ref)


def _float_key(v):
    bits = lax.bitcast_convert_type(v, jnp.int32)
    return bits ^ ((bits >> 31) & 0x7FFFFFFF)


def _key_float(k):
    return lax.bitcast_convert_type(k ^ ((k >> 31) & 0x7FFFFFFF), F32)


def _any(mask):
    return jnp.max(jnp.where(mask, 1.0, 0.0)) > 0.5


def _dsa_kernel(topk, qbt_ref, kb_ref, vbt_ref, qit_ref, ki_ref, wit_ref, zb_ref, bias_ref, o_ref,
                sc_ref, qm_ref, qim_ref, sel_ref, m_ref, mb_ref, acc_ref, y_ref, *s_refs):
    i = pl.program_id(1)
    adm = _diag_admissible()
    n_chunks = TILE // ROW_CHUNK

    row = lax.broadcasted_iota(jnp.int32, (LANES, TILE), 0)
    for h in range(DSA_HEADS):
        g = h // 2
        qm_ref[h] = jnp.where((row // DSA_HEAD_DIM) == (h % 2),
                              qbt_ref[0, LANES * g:LANES * (g + 1), :], 0).astype(MXU_DTYPE)
        gi = h // 4
        qim_ref[h] = jnp.where((row // IDX_DIM) == (h % 4),
                               qit_ref[0, LANES * gi:LANES * (gi + 1), :], 0).astype(MXU_DTYPE)

    def fold8(op, acc, v):
        for r in range(v.shape[0] // 8):
            acc = op(acc, v[8 * r:8 * (r + 1)])
        return acc

    def score_block(j, masked, lohi):
        off = pl.multiple_of(j * TILE, TILE)
        kij = ki_ref[0, pl.ds(off, TILE), :]
        tot = jnp.zeros((TILE, TILE), F32)
        for h in range(IDX_HEADS):
            s = _nn(kij, qim_ref[h])
            tot = tot + jnp.maximum(s, 0.0) * wit_ref[0, h:h + 1, :]
        lo8, hi8 = lohi
        if masked:
            lo8 = fold8(jnp.minimum, lo8, jnp.where(adm, tot, -NEG_INF))
            tot = jnp.where(adm, tot, NEG_INF)
        else:
            lo8 = fold8(jnp.minimum, lo8, tot)
        hi8 = fold8(jnp.maximum, hi8, tot)
        sc_ref[pl.ds(off, TILE), :] = tot
        return lo8, hi8

    lohi = (jnp.full((8, TILE), -NEG_INF, F32), jnp.full((8, TILE), NEG_INF, F32))
    lohi = lax.fori_loop(0, i, lambda j, c: score_block(j, False, c), lohi)
    lo8, hi8 = score_block(i, True, lohi)
    smin = jnp.min(lo8, axis=0, keepdims=True)
    smax = jnp.max(hi8, axis=0, keepdims=True)

    def count(pred):
        def body(jb, acc):
            for c in range(n_chunks):
                off = pl.multiple_of(jb * TILE + c * ROW_CHUNK, ROW_CHUNK)
                acc = acc + jnp.where(pred(sc_ref[pl.ds(off, ROW_CHUNK), :], off), 1.0, 0.0)
            return acc
        acc = lax.fori_loop(0, i + 1, body, jnp.zeros((ROW_CHUNK, TILE), F32))
        return jnp.sum(acc, axis=0, keepdims=True)

    def mid_of(lo, hi):
        return (lo >> 1) + (hi >> 1) + (lo & hi & 1)

    kf = float(topk)
    qpos = i * TILE + lax.broadcasted_iota(jnp.int32, (1, TILE), 1)
    n_adm = ((qpos // CHUNK + 1) * CHUNK).astype(F32)
    few = n_adm <= kf

    def one_pass(state):
        lo, hi, cnt = state
        active = mid_of(lo, hi) != lo
        quarter = (hi >> 2) - (lo >> 2)
        probe = _float_key(0.5 * _key_float(lo) + 0.5 * _key_float(hi))
        probe = jnp.clip(probe, lo + quarter, hi - quarter)
        probe = jnp.clip(probe, lo + 1, hi - 1)
        thr = _key_float(probe)
        c = count(lambda blk, off: blk >= thr)
        ge = c >= kf
        take_lo = jnp.logical_and(active, ge)
        new_hi = jnp.where(jnp.logical_and(active, jnp.logical_not(ge)), probe, hi)
        new_hi = jnp.where(jnp.logical_and(active, c == kf), probe + 1, new_hi)
        return jnp.where(take_lo, probe, lo), new_hi, jnp.where(take_lo, c, cnt)

    def b_cond(c):
        it, lo, hi, _ = c
        return jnp.logical_and(it < 128, _any(mid_of(lo, hi) != lo))

    def b_body(c):
        it, lo, hi, cnt = c
        lo, hi, cnt = one_pass(one_pass((lo, hi, cnt)))
        return it + 2, lo, hi, cnt

    lo0 = jnp.where(few, KEY_LO, _float_key(smin))
    hi0 = jnp.where(few, KEY_LO + 1, _float_key(smax) + 1)
    _, lo, _, cnt = lax.while_loop(b_cond, b_body, (jnp.int32(0), lo0, hi0, n_adm))
    thr = _key_float(lo)

    excess = cnt > kf

    @pl.when(_any(excess))
    def _():
        need = kf - count(lambda blk, off: blk > thr)

        def tie_le(jmax):
            def pred(blk, off):
                idx = off + lax.broadcasted_iota(jnp.int32, (ROW_CHUNK, TILE), 0)
                return jnp.logical_and(blk == thr, idx <= jmax)
            return count(pred)

        def j_body(_, c):
            jlo, jhi = c
            jmid = (jlo + jhi) >> 1
            ok = tie_le(jmid) >= need
            return jnp.where(ok, jlo, jmid), jnp.where(ok, jmid, jhi)

        n_keys = (i + 1) * TILE
        j0 = (jnp.full((1, TILE), -1, jnp.int32), jnp.zeros((1, TILE), jnp.int32) + (n_keys - 1))
        _, jcut = lax.fori_loop(0, 14, j_body, j0)

        def drop(c, carry):
            off = pl.multiple_of(c * ROW_CHUNK, ROW_CHUNK)
            blk = sc_ref[pl.ds(off, ROW_CHUNK), :]
            idx = off + lax.broadcasted_iota(jnp.int32, (ROW_CHUNK, TILE), 0)
            kill = jnp.logical_and(jnp.logical_and(blk == thr, idx > jcut), excess)
            sc_ref[pl.ds(off, ROW_CHUNK), :] = jnp.where(kill, NEG_INF, blk)
            return carry

        lax.fori_loop(0, (i + 1) * n_chunks, drop, 0)

    _flash_init(m_ref, acc_ref)

    def select(j, slot):
        off = pl.multiple_of(j * TILE, TILE)
        sel_ref[slot] = jnp.where(sc_ref[pl.ds(off, TILE), :] >= thr, 0.0, NEG_INF)

    def score(h, j, slot, near=None):
        g = h // 2
        off = pl.multiple_of(j * TILE, TILE)
        s = _nn(kb_ref[0, pl.ds(off, TILE), LANES * g:LANES * (g + 1)], qm_ref[h])
        if near is not None:
            s = s + bias_ref[h, near]
        return s + sel_ref[slot]

    def value(h, j):
        return vbt_ref[0, j, V_ROWS * h:V_ROWS * (h + 1), :]

    flash = _Flash(DSA_HEADS, score, value, m_ref, mb_ref, acc_ref, s_refs, prepare=select)
    flash.block(i, functools.partial(score, near=1))

    @pl.when(i >= 1)
    def _():
        flash.block(i - 1, functools.partial(score, near=0))

    flash.pipeline(i - 1)
    _flash_finish(DSA_HEADS, zb_ref, o_ref, acc_ref, y_ref)


def _merge_kernel(x_ref, ya_ref, yb_ref, ga_ref, gb_ref, woa_ref, wob_ref, wout_ref, fg_ref, o_ref):
    merged = (ga_ref[0].astype(F32) * _nn(ya_ref[0], woa_ref[...])
              + gb_ref[0].astype(F32) * _nn(yb_ref[0], wob_ref[...]))
    y = x_ref[0] + _nn(merged.astype(MXU_DTYPE), wout_ref[...])
    o_ref[0] = _rms(y, fg_ref[...])


def _t5_bucket(rel):
    nb = REL_BUCKETS // 2
    max_exact = nb // 2
    ret = (rel > 0).astype(jnp.int32) * nb
    n = jnp.abs(rel)
    nf = jnp.maximum(n, 1).astype(jnp.float32)
    large = max_exact + (jnp.log(nf / max_exact) / np.log(REL_MAX_DIST / max_exact)
                         * (nb - max_exact)).astype(jnp.int32)
    large = jnp.minimum(large, nb - 1)
    return ret + jnp.where(n < max_exact, n, large)


def _bias_tables(rel_bias):
    key = jnp.arange(TILE, dtype=jnp.int32)[:, None]
    qry = jnp.arange(TILE, dtype=jnp.int32)[None, :]
    rel_diag = key - qry
    far = rel_bias[_t5_bucket(jnp.int32(-(TILE + 1)))]

    def table(rel):
        bucket = _t5_bucket(rel)[None]
        out = jnp.zeros((DSA_HEADS, TILE, TILE), F32)
        for b in range(REL_BUCKETS):
            out = jnp.where(bucket == b, rel_bias[b][:, None, None], out)
        return (out - far[:, None, None]) * LOG2E

    diag = jnp.where(((key // CHUNK) <= (qry // CHUNK))[None], table(rel_diag), NEG_INF)
    return jnp.stack([table(rel_diag - TILE), diag], axis=1).astype(F32)


def _rope_tables(seq):
    half = MLA_ROPE // 2
    freqs = ROPE_BASE ** (-jnp.arange(half, dtype=jnp.float32) / half)
    ang = jnp.arange(seq, dtype=jnp.int32).astype(jnp.float32)[:, None] * freqs[None, :]
    cos, sin = jnp.cos(ang), jnp.sin(ang)
    zeros = jnp.zeros((seq, LANES - MLA_NOPE - MLA_ROPE), F32)
    lead = jnp.zeros((seq, MLA_NOPE), F32)
    c_k = jnp.concatenate([lead, cos, cos, zeros], axis=1)
    s_k = jnp.concatenate([lead, -sin, sin, zeros], axis=1)
    scale = (MLA_NOPE + MLA_ROPE) ** -0.5 * LOG2E
    c_q = jnp.concatenate([lead + 1.0, cos, cos, zeros], axis=1) * scale
    return c_q.T, (s_k * scale).T, c_k, s_k


def _prep_weights(w_in, w_uq, w_ukv):
    cuts = np.cumsum(IN_SPLITS)[:-1].tolist()
    (w_qlat, w_ckv, w_kr, w_za, w_qb, w_kb, w_vb, w_zb,
     w_qi, w_ki, w_wi, w_ga, w_gb) = jnp.split(w_in, cuts, axis=1)
    half = MLA_ROPE // 2
    swap = np.concatenate([np.arange(half, MLA_ROPE), np.arange(half)])

    def z(n):
        return jnp.zeros((D_MODEL, n), F32)

    pad = LANES - MLA_NOPE - MLA_ROPE
    kr = jnp.concatenate([z(MLA_NOPE), w_kr, z(pad)], axis=1)
    krs = jnp.concatenate([z(MLA_NOPE), w_kr[:, swap], z(pad)], axis=1)
    ki4 = jnp.tile(w_ki, (1, LANES // IDX_DIM))
    w1 = jnp.concatenate([w_qlat, w_ckv, kr, krs, w_za, w_kb, w_zb, ki4, w_ga, w_gb], axis=1)
    w2t = jnp.concatenate([w_qb, w_vb, w_qi, w_wi, z(BF16_ROWS - IDX_HEADS)], axis=1).T

    wuq = w_uq.reshape(MLA_Q_RANK, MLA_HEADS, MLA_NOPE + MLA_ROPE)
    zq = jnp.zeros((MLA_Q_RANK, MLA_HEADS, pad), F32)
    wuqm = jnp.concatenate([wuq, zq], axis=2).reshape(MLA_Q_RANK, MLA_HEADS * LANES)
    wuqs = jnp.concatenate([jnp.zeros((MLA_Q_RANK, MLA_HEADS, MLA_NOPE), F32),
                            wuq[:, :, MLA_NOPE:][:, :, swap], zq], axis=2
                           ).reshape(MLA_Q_RANK, MLA_HEADS * LANES)
    wukv = w_ukv.reshape(MLA_KV_RANK, MLA_HEADS, MLA_NOPE + MLA_V)
    wk = jnp.concatenate([wukv[:, :, :MLA_NOPE],
                          jnp.zeros((MLA_KV_RANK, MLA_HEADS, LANES - MLA_NOPE), F32)], axis=2
                         ).reshape(MLA_KV_RANK, MLA_HEADS * LANES)
    wv = wukv[:, :, MLA_NOPE:].reshape(MLA_KV_RANK, MLA_WIDTH)
    c = lambda a: a.astype(MXU_DTYPE)
    return c(w1), c(w2t), c(wuqm.T), c(wuqs.T), c(wk), c(wv.T)


def _full(shape):
    return pl.BlockSpec(shape, lambda *_: (0,) * len(shape))


def _resident(shape, index_map):
    return pl.BlockSpec(shape, index_map, pipeline_mode=pl.Buffered(1))


def kernel(x, norm_g, w_in, g_q_lat, w_uq, g_kv_lat, w_ukv, w_o_a, w_o_b, w_out, rel_bias, final_g):
    B, S, D = x.shape
    assert D == D_MODEL and S % TILE == 0 and norm_g.shape[0] == 1
    assert MLA_V == HEAD_V and DSA_HEAD_DIM == HEAD_V
    nb = S // TILE
    topk = min(TOPK_MAX, S // 4)
    bf = MXU_DTYPE
    params = functools.partial(pltpu.CompilerParams, vmem_limit_bytes=VMEM_LIMIT)

    w1, w2t, wuqm_t, wuqs_t, wk, wv_t = _prep_weights(w_in[0], w_uq[0], w_ukv[0])
    cq_t, sq_t, c_k, s_k = _rope_tables(S)
    n1, n2 = w1.shape[1], w2t.shape[0]
    va_rows, vb_rows = MLA_HEADS * V_ROWS, DSA_HEADS * V_ROWS

    tok = lambda c: pl.BlockSpec((1, TILE, c), lambda b, r: (b, r, 0))
    tok_t = lambda c: pl.BlockSpec((1, c, TILE), lambda b, r: (b, 0, r))
    blk_t = lambda c: pl.BlockSpec((1, 1, c, TILE), lambda b, r: (b, r, 0, 0))
    sds = jax.ShapeDtypeStruct
    (qa_t, ka, va_t, za, qb_t, kb, vb_t, zb, qi_t, ki, wi_t, ga, gb) = pl.pallas_call(
        _proj_kernel,
        grid=(B, nb),
        in_specs=[tok(D), _full((1, D)), _full((D, n1)), _full((n2, D)),
                  _full((1, MLA_Q_RANK)), _full((MLA_HEADS * LANES, MLA_Q_RANK)),
                  _full((MLA_HEADS * LANES, MLA_Q_RANK)),
                  _full((1, MLA_KV_RANK)), _full((MLA_KV_RANK, MLA_HEADS * LANES)),
                  _full((MLA_WIDTH, MLA_KV_RANK)),
                  pl.BlockSpec((LANES, TILE), lambda b, r: (0, r)),
                  pl.BlockSpec((LANES, TILE), lambda b, r: (0, r)),
                  pl.BlockSpec((TILE, LANES), lambda b, r: (r, 0)),
                  pl.BlockSpec((TILE, LANES), lambda b, r: (r, 0))],
        out_specs=[tok_t(MLA_HEADS * LANES), tok(MLA_HEADS * LANES), blk_t(va_rows), tok(MLA_WIDTH),
                   tok_t(DSA_WIDTH), tok(DSA_WIDTH), blk_t(vb_rows), tok(DSA_WIDTH),
                   tok_t(IDX_HEADS * IDX_DIM), tok(LANES), tok_t(IDX_HEADS), tok(D), tok(D)],
        out_shape=[sds((B, MLA_HEADS * LANES, S), bf), sds((B, S, MLA_HEADS * LANES), bf),
                   sds((B, nb, va_rows, TILE), bf), sds((B, S, MLA_WIDTH), bf),
                   sds((B, DSA_WIDTH, S), bf), sds((B, S, DSA_WIDTH), bf),
                   sds((B, nb, vb_rows, TILE), bf), sds((B, S, DSA_WIDTH), bf),
                   sds((B, IDX_HEADS * IDX_DIM, S), bf), sds((B, S, LANES), bf),
                   sds((B, IDX_HEADS, S), F32), sds((B, S, D), bf), sds((B, S, D), bf)],
        compiler_params=params(dimension_semantics=("parallel", "parallel")),
        name="proj",
    )(x, norm_g, w1, w2t, g_q_lat, wuqm_t, wuqs_t, g_kv_lat, wk, wv_t, cq_t, sq_t, c_k, s_k)

    def flash_state(heads):
        return ([pltpu.VMEM((heads, 1, TILE), F32),
                 pltpu.VMEM((heads, 2, 1, TILE), F32),
                 pltpu.VMEM((heads, V_ROWS, TILE), F32),
                 pltpu.VMEM((heads * HEAD_V, TILE), F32)]
                + [pltpu.VMEM((2, TILE, TILE), F32)] * heads)

    ya = pl.pallas_call(
        _mla_kernel,
        grid=(B, nb),
        in_specs=[pl.BlockSpec((1, MLA_HEADS * LANES, TILE), lambda b, i: (b, 0, i)),
                  _resident((1, S, MLA_HEADS * LANES), lambda b, i: (b, 0, 0)),
                  _resident((1, nb, va_rows, TILE), lambda b, i: (b, 0, 0, 0)),
                  pl.BlockSpec((1, TILE, MLA_WIDTH), lambda b, i: (b, i, 0))],
        out_specs=pl.BlockSpec((1, TILE, MLA_WIDTH), lambda b, i: (b, i, 0)),
        out_shape=sds((B, S, MLA_WIDTH), bf),
        scratch_shapes=flash_state(MLA_HEADS),
        compiler_params=params(dimension_semantics=("parallel", "arbitrary")),
        name="mla",
    )(qa_t, ka, va_t, za)

    yb = pl.pallas_call(
        functools.partial(_dsa_kernel, topk),
        grid=(B, nb),
        in_specs=[pl.BlockSpec((1, DSA_WIDTH, TILE), lambda b, i: (b, 0, i)),
                  _resident((1, S, DSA_WIDTH), lambda b, i: (b, 0, 0)),
                  _resident((1, nb, vb_rows, TILE), lambda b, i: (b, 0, 0, 0)),
                  pl.BlockSpec((1, IDX_HEADS * IDX_DIM, TILE), lambda b, i: (b, 0, i)),
                  _resident((1, S, LANES), lambda b, i: (b, 0, 0)),
                  pl.BlockSpec((1, IDX_HEADS, TILE), lambda b, i: (b, 0, i)),
                  pl.BlockSpec((1, TILE, DSA_WIDTH), lambda b, i: (b, i, 0)),
                  _resident((DSA_HEADS, 2, TILE, TILE), lambda b, i: (0, 0, 0, 0))],
        out_specs=pl.BlockSpec((1, TILE, DSA_WIDTH), lambda b, i: (b, i, 0)),
        out_shape=sds((B, S, DSA_WIDTH), bf),
        scratch_shapes=[pltpu.VMEM((S, TILE), F32),
                        pltpu.VMEM((DSA_HEADS, LANES, TILE), bf),
                        pltpu.VMEM((IDX_HEADS, LANES, TILE), bf),
                        pltpu.VMEM((2, TILE, TILE), F32),
                        *flash_state(DSA_HEADS)],
        compiler_params=params(dimension_semantics=("parallel", "arbitrary")),
        name="dsa",
    )(qb_t, kb, vb_t, qi_t, ki, wi_t, zb, _bias_tables(rel_bias))

    return pl.pallas_call(
        _merge_kernel,
        grid=(B, nb),
        in_specs=[tok(D), tok(MLA_WIDTH), tok(DSA_WIDTH), tok(D), tok(D),
                  _full((MLA_WIDTH, D)), _full((DSA_WIDTH, D)), _full((D, D)), _full((1, D))],
        out_specs=tok(D),
        out_shape=sds((B, S, D), x.dtype),
        compiler_params=params(dimension_semantics=("parallel", "parallel")),
        name="merge",
    )(x, ya, yb, ga, gb, w_o_a[0].astype(bf), w_o_b[0].astype(bf), w_out[0].astype(bf),
      final_g.reshape(1, D))
```

```python
import functools
import math

import numpy as np
import jax
import jax.numpy as jnp
from jax import lax
from jax.experimental import pallas as pl
from jax.experimental.pallas import tpu as pltpu

D_MODEL = 1024
CHUNK = 64
EPS = 1e-6
MLA_HEADS = 8
MLA_NOPE = 64
MLA_ROPE = 32
MLA_V = 64
MLA_Q_RANK = 384
MLA_KV_RANK = 256
ROPE_BASE = 10000.0
MLA_WIDTH = MLA_HEADS * MLA_V
DSA_HEADS = 8
DSA_HEAD_DIM = 64
DSA_WIDTH = DSA_HEADS * DSA_HEAD_DIM
IDX_HEADS = 8
IDX_DIM = 32
TOPK_MAX = 256
REL_BUCKETS = 32
REL_MAX_DIST = 128
IN_SPLITS = (MLA_Q_RANK, MLA_KV_RANK, MLA_ROPE, MLA_WIDTH,
             DSA_WIDTH, DSA_WIDTH, DSA_WIDTH, DSA_WIDTH,
             IDX_HEADS * IDX_DIM, IDX_DIM, IDX_HEADS,
             D_MODEL, D_MODEL)

LANES = 128
BF16_ROWS = 16
TILE = 256
ROW_CHUNK = 64
HEAD_V = 64
V_ROWS = HEAD_V + BF16_ROWS
VMEM_LIMIT = 56 * 1024 * 1024
MXU_DTYPE = jnp.bfloat16
F32 = jnp.float32
NEG_INF = float("-inf")
M_INIT = -1e30
LOG2E = math.log2(math.e)

KEY_LO = int(np.int32(np.uint32(0x80800000)))


def _nt(a, b):
    return lax.dot_general(a, b, (((1,), (1,)), ((), ())), preferred_element_type=F32)


def _nn(a, b):
    return jnp.dot(a, b, preferred_element_type=F32)


def _sigmoid(v):
    return 1.0 / (1.0 + jnp.exp(-v))


def _rms(v, g):
    return v * lax.rsqrt(jnp.mean(v * v, axis=-1, keepdims=True) + EPS) * g


def _store_values(vt_ref, v_t):
    row = lax.broadcasted_iota(jnp.int32, (BF16_ROWS, TILE), 0)
    ones_row = jnp.where(row == 0, 1.0, 0.0).astype(MXU_DTYPE)
    for h in range(v_t.shape[0] // HEAD_V):
        vt_ref[0, 0, V_ROWS * h:V_ROWS * h + HEAD_V, :] = v_t[HEAD_V * h:HEAD_V * (h + 1)].astype(MXU_DTYPE)
        vt_ref[0, 0, V_ROWS * h + HEAD_V:V_ROWS * (h + 1), :] = ones_row


def _proj_kernel(x_ref, ng_ref, w1_ref, w2t_ref, gq_ref, wuqm_ref, wuqs_ref, gkv_ref, wk_ref,
                 wvt_ref, cqt_ref, sqt_ref, ck_ref, sk_ref,
                 qat_ref, ka_ref, vat_ref, za_ref, qbt_ref, kb_ref, vbt_ref, zb_ref,
                 qit_ref, ki_ref, wit_ref, ga_ref, gb_ref):
    hb = _rms(x_ref[0], ng_ref[...]).astype(MXU_DTYPE)

    def cols(lo, hi):
        return _nn(hb, w1_ref[:, lo:hi])

    def rows(lo, hi):
        return _nt(w2t_ref[lo:hi, :], hb)

    qn = _rms(cols(0, 384), gq_ref[...]).astype(MXU_DTYPE)
    qm = _nt(wuqm_ref[...], qn)
    qs = _nt(wuqs_ref[...], qn)
    cqt = cqt_ref[...]
    sqt = sqt_ref[...]
    for h in range(MLA_HEADS):
        sl = slice(LANES * h, LANES * (h + 1))
        qat_ref[0, sl, :] = (qm[sl] * cqt + qs[sl] * sqt).astype(MXU_DTYPE)

    kvn = _rms(cols(384, 640), gkv_ref[...]).astype(MXU_DTYPE)
    kn = _nn(kvn, wk_ref[...])
    kpe = cols(640, 768) * ck_ref[...] + cols(768, 896) * sk_ref[...]
    for h in range(MLA_HEADS):
        sl = slice(LANES * h, LANES * (h + 1))
        ka_ref[0, :, sl] = (kn[:, sl] + kpe).astype(MXU_DTYPE)
    _store_values(vat_ref, _nt(wvt_ref[...], kvn))

    za = cols(896, 1408)
    za_ref[0] = (za * _sigmoid(za)).astype(MXU_DTYPE)
    kb_ref[0] = cols(1408, 1920).astype(MXU_DTYPE)
    zb = cols(1920, 2432)
    zb_ref[0] = (zb * _sigmoid(zb)).astype(MXU_DTYPE)
    ki_ref[0] = cols(2432, 2560).astype(MXU_DTYPE)
    ga_ref[0] = _sigmoid(cols(2560, 3584)).astype(MXU_DTYPE)
    gb_ref[0] = _sigmoid(cols(3584, 4608)).astype(MXU_DTYPE)

    qbt_ref[0] = (rows(0, 512) * (DSA_HEAD_DIM ** -0.5 * LOG2E)).astype(MXU_DTYPE)
    _store_values(vbt_ref, rows(512, 1024))
    qit_ref[0] = rows(1024, 1280).astype(MXU_DTYPE)
    wit_ref[0] = rows(1280, 1296)[:IDX_HEADS] * ((IDX_DIM * IDX_HEADS) ** -0.5)


def _diag_admissible():
    key = lax.broadcasted_iota(jnp.int32, (TILE, TILE), 0)
    qry = lax.broadcasted_iota(jnp.int32, (TILE, TILE), 1)
    return (key // CHUNK) <= (qry // CHUNK)


def _flash_init(m_ref, acc_ref):
    m_ref[...] = jnp.full(m_ref.shape, M_INIT, F32)
    acc_ref[...] = jnp.zeros(acc_ref.shape, F32)


class _Flash:
    def __init__(self, n, score, value, m_ref, mb_ref, acc_ref, s_refs, prepare=None):
        self.n, self.score, self.value, self.prepare = n, score, value, prepare
        self.m_ref, self.mb_ref, self.acc_ref, self.s_refs = m_ref, mb_ref, acc_ref, s_refs

    def put_scores(self, h, slot, s):
        self.s_refs[h][slot] = s
        mx = s[0:ROW_CHUNK]
        for c in range(1, TILE // ROW_CHUNK):
            mx = jnp.maximum(mx, s[ROW_CHUNK * c:ROW_CHUNK * (c + 1)])
        self.mb_ref[h, slot] = jnp.max(mx, axis=0, keepdims=True)

    def scores(self, j, slot, score=None):
        if self.prepare is not None:
            self.prepare(j, slot)
        for h in range(self.n):
            self.put_scores(h, slot, (score or self.score)(h, j, slot))

    def softmax_values(self, j, slot, nxt=None):
        rows = ROW_CHUNK
        n_chunks = TILE // rows
        if nxt is not None and self.prepare is not None:
            self.prepare(*nxt)
        for h in range(self.n):
            if nxt is not None:
                self.put_scores(h, nxt[1], self.score(h, *nxt))
            s_ref = self.s_refs[h]
            m_old = self.m_ref[h]
            m_new = jnp.maximum(m_old, self.mb_ref[h, slot])
            p = jnp.concatenate(
                [jnp.exp2(s_ref[slot, rows * c:rows * (c + 1)] - m_new).astype(MXU_DTYPE)
                 for c in range(n_chunks)], axis=0)
            self.m_ref[h] = m_new
            alpha = jnp.exp2(m_old - m_new)
            self.acc_ref[h] = alpha * self.acc_ref[h] + _nn(self.value(h, j), p)

    def block(self, j, score):
        self.scores(j, 0, score)
        self.softmax_values(j, 0)

    def pipeline(self, n_blocks):
        _pipeline(n_blocks, lambda: self.scores(0, 0),
                  lambda j, slot: self.softmax_values(j, slot, nxt=(j + 1, 1 - slot)),
                  self.softmax_values)


def _pipeline(n_blocks, first, step, last):
    @pl.when(n_blocks > 0)
    def _():
        first()
        n_pairs = (n_blocks - 1) // 2

        def body(t, c):
            step(2 * t, 0)
            step(2 * t + 1, 1)
            return c

        lax.fori_loop(0, n_pairs, body, 0)
        j = 2 * n_pairs

        @pl.when(j + 1 < n_blocks)
        def _():
            step(j, 0)
            last(j + 1, 1)

        @pl.when(j + 1 == n_blocks)
        def _():
            last(j, 0)


def _flash_finish(n_heads, z_ref, o_ref, acc_ref, y_ref):
    for h in range(n_heads):
        y_ref[HEAD_V * h:HEAD_V * (h + 1), :] = acc_ref[h, 0:HEAD_V] / acc_ref[h, HEAD_V:HEAD_V + 1]
    o_ref[0] = (y_ref[...].T * z_ref[0].astype(F32)).astype(MXU_DTYPE)


def _mla_kernel(qt_ref, k_ref, vt_ref, z_ref, o_ref, m_ref, mb_ref, acc_ref, y_ref, *s_refs):
    i = pl.program_id(1)
    _flash_init(m_ref, acc_ref)

    def score(h, j, slot):
        sl = slice(LANES * h, LANES * (h + 1))
        off = pl.multiple_of(j * TILE, TILE)
        return _nn(k_ref[0, pl.ds(off, TILE), sl], qt_ref[0, sl, :])

    def diag_score(h, j, slot):
        return jnp.where(_diag_admissible(), score(h, j, slot), NEG_INF)

    def value(h, j):
        return vt_ref[0, j, V_ROWS * h:V_ROWS * (h + 1), :]

    flash = _Flash(MLA_HEADS, score, value, m_ref, mb_ref, acc_ref, s_refs)
    flash.block(i, diag_score)
    flash.pipeline(i)
    _flash_finish(MLA_HEADS, z_ref, o_ref, acc_ref, y_ref)


def _float_key(v):
    bits = lax.bitcast_convert_type(v, jnp.int32)
    return bits ^ ((bits >> 31) & 0x7FFFFFFF)


def _key_float(k):
    return lax.bitcast_convert_type(k ^ ((k >> 31) & 0x7FFFFFFF), F32)


def _any(mask):
    return jnp.max(jnp.where(mask, 1.0, 0.0)) > 0.5


def _dsa_kernel(topk, qbt_ref, kb_ref, vbt_ref, qit_ref, ki_ref, wit_ref, zb_ref, bias_ref, o_ref,
                sc_ref, qm_ref, qim_ref, sel_ref, lohi_ref, m_ref, mb_ref, acc_ref, y_ref, *s_refs):
    i = pl.program_id(1)
    adm = _diag_admissible()
    n_chunks = TILE // ROW_CHUNK

    row = lax.broadcasted_iota(jnp.int32, (LANES, TILE), 0)
    for h in range(DSA_HEADS):
        g = h // 2
        qm_ref[h] = jnp.where((row // DSA_HEAD_DIM) == (h % 2),
                              qbt_ref[0, LANES * g:LANES * (g + 1), :], 0).astype(MXU_DTYPE)
        gi = h // 4
        qim_ref[h] = jnp.where((row // IDX_DIM) == (h % 4),
                               qit_ref[0, LANES * gi:LANES * (gi + 1), :], 0).astype(MXU_DTYPE)

    def fold8(op, acc, v):
        for r in range(v.shape[0] // 8):
            acc = op(acc, v[8 * r:8 * (r + 1)])
        return acc

    heads_per_chunk = IDX_HEADS // n_chunks

    def stage_dots(j, slot, heads):
        off = pl.multiple_of(j * TILE, TILE)
        kij = ki_ref[0, pl.ds(off, TILE), :]
        for h in heads:
            s_refs[h][slot] = _nn(kij, qim_ref[h])

    def fold_scores(j, slot, masked=False, nxt=None):
        off = pl.multiple_of(j * TILE, TILE)
        lo8, hi8 = lohi_ref[0], lohi_ref[1]
        for c in range(n_chunks):
            if nxt is not None:
                stage_dots(*nxt, range(heads_per_chunk * c, heads_per_chunk * (c + 1)))
            rs = slice(ROW_CHUNK * c, ROW_CHUNK * (c + 1))
            tot = jnp.zeros((ROW_CHUNK, TILE), F32)
            for h in range(IDX_HEADS):
                tot = tot + jnp.maximum(s_refs[h][slot, rs], 0.0) * wit_ref[0, h:h + 1, :]
            if masked:
                lo8 = fold8(jnp.minimum, lo8, jnp.where(adm[rs], tot, -NEG_INF))
                tot = jnp.where(adm[rs], tot, NEG_INF)
            else:
                lo8 = fold8(jnp.minimum, lo8, tot)
            hi8 = fold8(jnp.maximum, hi8, tot)
            sc_ref[pl.ds(pl.multiple_of(off + ROW_CHUNK * c, ROW_CHUNK), ROW_CHUNK), :] = tot
        lohi_ref[0] = lo8
        lohi_ref[1] = hi8

    lohi_ref[0] = jnp.full((8, TILE), -NEG_INF, F32)
    lohi_ref[1] = jnp.full((8, TILE), NEG_INF, F32)
    stage_dots(i, 0, range(IDX_HEADS))
    fold_scores(i, 0, masked=True)
    _pipeline(i, lambda: stage_dots(0, 0, range(IDX_HEADS)),
              lambda j, slot: fold_scores(j, slot, nxt=(j + 1, 1 - slot)),
              fold_scores)
    smin = jnp.min(lohi_ref[0], axis=0, keepdims=True)
    smax = jnp.max(lohi_ref[1], axis=0, keepdims=True)

    def count(pred):
        def body(jb, acc):
            for c in range(n_chunks):
                off = pl.multiple_of(jb * TILE + c * ROW_CHUNK, ROW_CHUNK)
                acc = acc + jnp.where(pred(sc_ref[pl.ds(off, ROW_CHUNK), :], off), 1.0, 0.0)
            return acc
        acc = lax.fori_loop(0, i + 1, body, jnp.zeros((ROW_CHUNK, TILE), F32))
        return jnp.sum(acc, axis=0, keepdims=True)

    def mid_of(lo, hi):
        return (lo >> 1) + (hi >> 1) + (lo & hi & 1)

    kf = float(topk)
    qpos = i * TILE + lax.broadcasted_iota(jnp.int32, (1, TILE), 1)
    n_adm = ((qpos // CHUNK + 1) * CHUNK).astype(F32)
    few = n_adm <= kf

    def one_pass(state):
        lo, hi, cnt = state
        active = mid_of(lo, hi) != lo
        margin = (hi >> 3) - (lo >> 3)
        probe = _float_key(0.5 * _key_float(lo) + 0.5 * _key_float(hi))
        probe = jnp.clip(probe, lo + margin, hi - margin)
        probe = jnp.clip(probe, lo + 1, hi - 1)
        thr = _key_float(probe)
        c = count(lambda blk, off: blk >= thr)
        ge = c >= kf
        take_lo = jnp.logical_and(active, ge)
        new_hi = jnp.where(jnp.logical_and(active, jnp.logical_not(ge)), probe, hi)
        new_hi = jnp.where(jnp.logical_and(active, c == kf), probe + 1, new_hi)
        return jnp.where(take_lo, probe, lo), new_hi, jnp.where(take_lo, c, cnt)

    def b_cond(c):
        it, lo, hi, _ = c
        return jnp.logical_and(it < 128, _any(mid_of(lo, hi) != lo))

    def b_body(c):
        it, lo, hi, cnt = c
        lo, hi, cnt = one_pass(one_pass((lo, hi, cnt)))
        return it + 2, lo, hi, cnt

    c_pos = count(lambda blk, off: blk > 0.0)
    c_nonneg = count(lambda blk, off: blk >= 0.0)
    at_zero = jnp.logical_and(c_pos < kf, kf <= c_nonneg)
    above = kf <= c_pos
    lo0 = jnp.where(at_zero, 0, jnp.where(above, 1, _float_key(smin)))
    hi0 = jnp.where(at_zero, 1, jnp.where(above, _float_key(smax) + 1, -1))
    cnt0 = jnp.where(at_zero, c_nonneg, jnp.where(above, c_pos, n_adm))
    lo0 = jnp.where(few, KEY_LO, lo0)
    hi0 = jnp.where(few, KEY_LO + 1, hi0)
    cnt0 = jnp.where(few, n_adm, cnt0)
    _, lo, _, cnt = lax.while_loop(b_cond, b_body, (jnp.int32(0), lo0, hi0, cnt0))
    thr = _key_float(lo)

    excess = cnt > kf

    @pl.when(_any(excess))
    def _():
        need = kf - count(lambda blk, off: blk > thr)

        def tie_le(jmax):
            def pred(blk, off):
                idx = off + lax.broadcasted_iota(jnp.int32, (ROW_CHUNK, TILE), 0)
                return jnp.logical_and(blk == thr, idx <= jmax)
            return count(pred)

        def j_body(_, c):
            jlo, jhi = c
            jmid = (jlo + jhi) >> 1
            ok = tie_le(jmid) >= need
            return jnp.where(ok, jlo, jmid), jnp.where(ok, jmid, jhi)

        n_keys = (i + 1) * TILE
        j0 = (jnp.full((1, TILE), -1, jnp.int32), jnp.zeros((1, TILE), jnp.int32) + (n_keys - 1))
        _, jcut = lax.fori_loop(0, 14, j_body, j0)

        def drop(c, carry):
            off = pl.multiple_of(c * ROW_CHUNK, ROW_CHUNK)
            blk = sc_ref[pl.ds(off, ROW_CHUNK), :]
            idx = off + lax.broadcasted_iota(jnp.int32, (ROW_CHUNK, TILE), 0)
            kill = jnp.logical_and(jnp.logical_and(blk == thr, idx > jcut), excess)
            sc_ref[pl.ds(off, ROW_CHUNK), :] = jnp.where(kill, NEG_INF, blk)
            return carry

        lax.fori_loop(0, (i + 1) * n_chunks, drop, 0)

    _flash_init(m_ref, acc_ref)

    def select(j, slot):
        off = pl.multiple_of(j * TILE, TILE)
        sel_ref[slot] = jnp.where(sc_ref[pl.ds(off, TILE), :] >= thr, 0.0, NEG_INF)

    def score(h, j, slot, near=None):
        g = h // 2
        off = pl.multiple_of(j * TILE, TILE)
        s = _nn(kb_ref[0, pl.ds(off, TILE), LANES * g:LANES * (g + 1)], qm_ref[h])
        if near is not None:
            s = s + bias_ref[h, near]
        return s + sel_ref[slot]

    def value(h, j):
        return vbt_ref[0, j, V_ROWS * h:V_ROWS * (h + 1), :]

    flash = _Flash(DSA_HEADS, score, value, m_ref, mb_ref, acc_ref, s_refs, prepare=select)
    flash.block(i, functools.partial(score, near=1))

    @pl.when(i >= 1)
    def _():
        flash.block(i - 1, functools.partial(score, near=0))

    flash.pipeline(i - 1)
    _flash_finish(DSA_HEADS, zb_ref, o_ref, acc_ref, y_ref)


def _merge_kernel(x_ref, ya_ref, yb_ref, ga_ref, gb_ref, woa_ref, wob_ref, wout_ref, fg_ref, o_ref):
    merged = (ga_ref[0].astype(F32) * _nn(ya_ref[0], woa_ref[...])
              + gb_ref[0].astype(F32) * _nn(yb_ref[0], wob_ref[...]))
    y = x_ref[0] + _nn(merged.astype(MXU_DTYPE), wout_ref[...])
    o_ref[0] = _rms(y, fg_ref[...])


def _t5_bucket(rel):
    nb = REL_BUCKETS // 2
    max_exact = nb // 2
    ret = (rel > 0).astype(jnp.int32) * nb
    n = jnp.abs(rel)
    nf = jnp.maximum(n, 1).astype(jnp.float32)
    large = max_exact + (jnp.log(nf / max_exact) / np.log(REL_MAX_DIST / max_exact)
                         * (nb - max_exact)).astype(jnp.int32)
    large = jnp.minimum(large, nb - 1)
    return ret + jnp.where(n < max_exact, n, large)


def _bias_tables(rel_bias):
    key = jnp.arange(TILE, dtype=jnp.int32)[:, None]
    qry = jnp.arange(TILE, dtype=jnp.int32)[None, :]
    rel_diag = key - qry
    far = rel_bias[_t5_bucket(jnp.int32(-(TILE + 1)))]

    def table(rel):
        bucket = _t5_bucket(rel)[None]
        out = jnp.zeros((DSA_HEADS, TILE, TILE), F32)
        for b in range(REL_BUCKETS):
            out = jnp.where(bucket == b, rel_bias[b][:, None, None], out)
        return (out - far[:, None, None]) * LOG2E

    diag = jnp.where(((key // CHUNK) <= (qry // CHUNK))[None], table(rel_diag), NEG_INF)
    return jnp.stack([table(rel_diag - TILE), diag], axis=1).astype(F32)


def _rope_tables(seq):
    half = MLA_ROPE // 2
    freqs = ROPE_BASE ** (-jnp.arange(half, dtype=jnp.float32) / half)
    ang = jnp.arange(seq, dtype=jnp.int32).astype(jnp.float32)[:, None] * freqs[None, :]
    cos, sin = jnp.cos(ang), jnp.sin(ang)
    zeros = jnp.zeros((seq, LANES - MLA_NOPE - MLA_ROPE), F32)
    lead = jnp.zeros((seq, MLA_NOPE), F32)
    c_k = jnp.concatenate([lead, cos, cos, zeros], axis=1)
    s_k = jnp.concatenate([lead, -sin, sin, zeros], axis=1)
    scale = (MLA_NOPE + MLA_ROPE) ** -0.5 * LOG2E
    c_q = jnp.concatenate([lead + 1.0, cos, cos, zeros], axis=1) * scale
    return c_q.T, (s_k * scale).T, c_k, s_k


def _prep_weights(w_in, w_uq, w_ukv):
    cuts = np.cumsum(IN_SPLITS)[:-1].tolist()
    (w_qlat, w_ckv, w_kr, w_za, w_qb, w_kb, w_vb, w_zb,
     w_qi, w_ki, w_wi, w_ga, w_gb) = jnp.split(w_in, cuts, axis=1)
    half = MLA_ROPE // 2
    swap = np.concatenate([np.arange(half, MLA_ROPE), np.arange(half)])

    def z(n):
        return jnp.zeros((D_MODEL, n), F32)

    pad = LANES - MLA_NOPE - MLA_ROPE
    kr = jnp.concatenate([z(MLA_NOPE), w_kr, z(pad)], axis=1)
    krs = jnp.concatenate([z(MLA_NOPE), w_kr[:, swap], z(pad)], axis=1)
    ki4 = jnp.tile(w_ki, (1, LANES // IDX_DIM))
    w1 = jnp.concatenate([w_qlat, w_ckv, kr, krs, w_za, w_kb, w_zb, ki4, w_ga, w_gb], axis=1)
    w2t = jnp.concatenate([w_qb, w_vb, w_qi, w_wi, z(BF16_ROWS - IDX_HEADS)], axis=1).T

    wuq = w_uq.reshape(MLA_Q_RANK, MLA_HEADS, MLA_NOPE + MLA_ROPE)
    zq = jnp.zeros((MLA_Q_RANK, MLA_HEADS, pad), F32)
    wuqm = jnp.concatenate([wuq, zq], axis=2).reshape(MLA_Q_RANK, MLA_HEADS * LANES)
    wuqs = jnp.concatenate([jnp.zeros((MLA_Q_RANK, MLA_HEADS, MLA_NOPE), F32),
                            wuq[:, :, MLA_NOPE:][:, :, swap], zq], axis=2
                           ).reshape(MLA_Q_RANK, MLA_HEADS * LANES)
    wukv = w_ukv.reshape(MLA_KV_RANK, MLA_HEADS, MLA_NOPE + MLA_V)
    wk = jnp.concatenate([wukv[:, :, :MLA_NOPE],
                          jnp.zeros((MLA_KV_RANK, MLA_HEADS, LANES - MLA_NOPE), F32)], axis=2
                         ).reshape(MLA_KV_RANK, MLA_HEADS * LANES)
    wv = wukv[:, :, MLA_NOPE:].reshape(MLA_KV_RANK, MLA_WIDTH)
    c = lambda a: a.astype(MXU_DTYPE)
    return c(w1), c(w2t), c(wuqm.T), c(wuqs.T), c(wk), c(wv.T)


def _full(shape):
    return pl.BlockSpec(shape, lambda *_: (0,) * len(shape))


def _resident(shape, index_map):
    return pl.BlockSpec(shape, index_map, pipeline_mode=pl.Buffered(1))


def kernel(x, norm_g, w_in, g_q_lat, w_uq, g_kv_lat, w_ukv, w_o_a, w_o_b, w_out, rel_bias, final_g):
    B, S, D = x.shape
    assert D == D_MODEL and S % TILE == 0 and norm_g.shape[0] == 1
    assert MLA_V == HEAD_V and DSA_HEAD_DIM == HEAD_V
    nb = S // TILE
    topk = min(TOPK_MAX, S // 4)
    bf = MXU_DTYPE
    params = functools.partial(pltpu.CompilerParams, vmem_limit_bytes=VMEM_LIMIT)

    w1, w2t, wuqm_t, wuqs_t, wk, wv_t = _prep_weights(w_in[0], w_uq[0], w_ukv[0])
    cq_t, sq_t, c_k, s_k = _rope_tables(S)
    n1, n2 = w1.shape[1], w2t.shape[0]
    va_rows, vb_rows = MLA_HEADS * V_ROWS, DSA_HEADS * V_ROWS

    tok = lambda c: pl.BlockSpec((1, TILE, c), lambda b, r: (b, r, 0))
    tok_t = lambda c: pl.BlockSpec((1, c, TILE), lambda b, r: (b, 0, r))
    blk_t = lambda c: pl.BlockSpec((1, 1, c, TILE), lambda b, r: (b, r, 0, 0))
    sds = jax.ShapeDtypeStruct
    (qa_t, ka, va_t, za, qb_t, kb, vb_t, zb, qi_t, ki, wi_t, ga, gb) = pl.pallas_call(
        _proj_kernel,
        grid=(B, nb),
        in_specs=[tok(D), _full((1, D)), _full((D, n1)), _full((n2, D)),
                  _full((1, MLA_Q_RANK)), _full((MLA_HEADS * LANES, MLA_Q_RANK)),
                  _full((MLA_HEADS * LANES, MLA_Q_RANK)),
                  _full((1, MLA_KV_RANK)), _full((MLA_KV_RANK, MLA_HEADS * LANES)),
                  _full((MLA_WIDTH, MLA_KV_RANK)),
                  pl.BlockSpec((LANES, TILE), lambda b, r: (0, r)),
                  pl.BlockSpec((LANES, TILE), lambda b, r: (0, r)),
                  pl.BlockSpec((TILE, LANES), lambda b, r: (r, 0)),
                  pl.BlockSpec((TILE, LANES), lambda b, r: (r, 0))],
        out_specs=[tok_t(MLA_HEADS * LANES), tok(MLA_HEADS * LANES), blk_t(va_rows), tok(MLA_WIDTH),
                   tok_t(DSA_WIDTH), tok(DSA_WIDTH), blk_t(vb_rows), tok(DSA_WIDTH),
                   tok_t(IDX_HEADS * IDX_DIM), tok(LANES), tok_t(IDX_HEADS), tok(D), tok(D)],
        out_shape=[sds((B, MLA_HEADS * LANES, S), bf), sds((B, S, MLA_HEADS * LANES), bf),
                   sds((B, nb, va_rows, TILE), bf), sds((B, S, MLA_WIDTH), bf),
                   sds((B, DSA_WIDTH, S), bf), sds((B, S, DSA_WIDTH), bf),
                   sds((B, nb, vb_rows, TILE), bf), sds((B, S, DSA_WIDTH), bf),
                   sds((B, IDX_HEADS * IDX_DIM, S), bf), sds((B, S, LANES), bf),
                   sds((B, IDX_HEADS, S), F32), sds((B, S, D), bf), sds((B, S, D), bf)],
        compiler_params=params(dimension_semantics=("parallel", "parallel")),
        name="proj",
    )(x, norm_g, w1, w2t, g_q_lat, wuqm_t, wuqs_t, g_kv_lat, wk, wv_t, cq_t, sq_t, c_k, s_k)

    def flash_state(heads):
        return ([pltpu.VMEM((heads, 1, TILE), F32),
                 pltpu.VMEM((heads, 2, 1, TILE), F32),
                 pltpu.VMEM((heads, V_ROWS, TILE), F32),
                 pltpu.VMEM((heads * HEAD_V, TILE), F32)]
                + [pltpu.VMEM((2, TILE, TILE), F32)] * heads)

    ya = pl.pallas_call(
        _mla_kernel,
        grid=(B, nb),
        in_specs=[pl.BlockSpec((1, MLA_HEADS * LANES, TILE), lambda b, i: (b, 0, i)),
                  _resident((1, S, MLA_HEADS * LANES), lambda b, i: (b, 0, 0)),
                  _resident((1, nb, va_rows, TILE), lambda b, i: (b, 0, 0, 0)),
                  pl.BlockSpec((1, TILE, MLA_WIDTH), lambda b, i: (b, i, 0))],
        out_specs=pl.BlockSpec((1, TILE, MLA_WIDTH), lambda b, i: (b, i, 0)),
        out_shape=sds((B, S, MLA_WIDTH), bf),
        scratch_shapes=flash_state(MLA_HEADS),
        compiler_params=params(dimension_semantics=("parallel", "arbitrary")),
        name="mla",
    )(qa_t, ka, va_t, za)

    yb = pl.pallas_call(
        functools.partial(_dsa_kernel, topk),
        grid=(B, nb),
        in_specs=[pl.BlockSpec((1, DSA_WIDTH, TILE), lambda b, i: (b, 0, i)),
                  _resident((1, S, DSA_WIDTH), lambda b, i: (b, 0, 0)),
                  _resident((1, nb, vb_rows, TILE), lambda b, i: (b, 0, 0, 0)),
                  pl.BlockSpec((1, IDX_HEADS * IDX_DIM, TILE), lambda b, i: (b, 0, i)),
                  _resident((1, S, LANES), lambda b, i: (b, 0, 0)),
                  pl.BlockSpec((1, IDX_HEADS, TILE), lambda b, i: (b, 0, i)),
                  pl.BlockSpec((1, TILE, DSA_WIDTH), lambda b, i: (b, i, 0)),
                  _resident((DSA_HEADS, 2, TILE, TILE), lambda b, i: (0, 0, 0, 0))],
        out_specs=pl.BlockSpec((1, TILE, DSA_WIDTH), lambda b, i: (b, i, 0)),
        out_shape=sds((B, S, DSA_WIDTH), bf),
        scratch_shapes=[pltpu.VMEM((S, TILE), F32),
                        pltpu.VMEM((DSA_HEADS, LANES, TILE), bf),
                        pltpu.VMEM((IDX_HEADS, LANES, TILE), bf),
                        pltpu.VMEM((2, TILE, TILE), F32),
                        pltpu.VMEM((2, 8, TILE), F32),
                        *flash_state(DSA_HEADS)],
        compiler_params=params(dimension_semantics=("parallel", "arbitrary")),
        name="dsa",
    )(qb_t, kb, vb_t, qi_t, ki, wi_t, zb, _bias_tables(rel_bias))

    return pl.pallas_call(
        _merge_kernel,
        grid=(B, nb),
        in_specs=[tok(D), tok(MLA_WIDTH), tok(DSA_WIDTH), tok(D), tok(D),
                  _full((MLA_WIDTH, D)), _full((DSA_WIDTH, D)), _full((D, D)), _full((1, D))],
        out_specs=tok(D),
        out_shape=sds((B, S, D), x.dtype),
        compiler_params=params(dimension_semantics=("parallel", "parallel")),
        name="merge",
    )(x, ya, yb, ga, gb, w_o_a[0].astype(bf), w_o_b[0].astype(bf), w_out[0].astype(bf),
      final_g.reshape(1, D))
```

```python
import functools
import math

import numpy as np
import jax
import jax.numpy as jnp
from jax import lax
from jax.experimental import pallas as pl
from jax.experimental.pallas import tpu as pltpu

D_MODEL = 1024
CHUNK = 64
EPS = 1e-6
MLA_HEADS = 8
MLA_NOPE = 64
MLA_ROPE = 32
MLA_V = 64
MLA_Q_RANK = 384
MLA_KV_RANK = 256
ROPE_BASE = 10000.0
MLA_WIDTH = MLA_HEADS * MLA_V
DSA_HEADS = 8
DSA_HEAD_DIM = 64
DSA_WIDTH = DSA_HEADS * DSA_HEAD_DIM
IDX_HEADS = 8
IDX_DIM = 32
TOPK_MAX = 256
REL_BUCKETS = 32
REL_MAX_DIST = 128
IN_SPLITS = (MLA_Q_RANK, MLA_KV_RANK, MLA_ROPE, MLA_WIDTH,
             DSA_WIDTH, DSA_WIDTH, DSA_WIDTH, DSA_WIDTH,
             IDX_HEADS * IDX_DIM, IDX_DIM, IDX_HEADS,
             D_MODEL, D_MODEL)

LANES = 128
BF16_ROWS = 16
TILE = 256
ROW_CHUNK = 64
HEAD_V = 64
V_ROWS = HEAD_V + BF16_ROWS
VMEM_LIMIT = 56 * 1024 * 1024
MXU_DTYPE = jnp.bfloat16
F32 = jnp.float32
NEG_INF = float("-inf")
M_INIT = -1e30
LOG2E = math.log2(math.e)

KEY_LO = int(np.int32(np.uint32(0x80800000)))
KEY_GRID = 1 << 16


def _nt(a, b):
    return lax.dot_general(a, b, (((1,), (1,)), ((), ())), preferred_element_type=F32)


def _nn(a, b):
    return jnp.dot(a, b, preferred_element_type=F32)


def _sigmoid(v):
    return 1.0 / (1.0 + jnp.exp(-v))


def _rms(v, g):
    return v * lax.rsqrt(jnp.mean(v * v, axis=-1, keepdims=True) + EPS) * g


def _store_values(vt_ref, v_t):
    row = lax.broadcasted_iota(jnp.int32, (BF16_ROWS, TILE), 0)
    ones_row = jnp.where(row == 0, 1.0, 0.0).astype(MXU_DTYPE)
    for h in range(v_t.shape[0] // HEAD_V):
        vt_ref[0, 0, V_ROWS * h:V_ROWS * h + HEAD_V, :] = v_t[HEAD_V * h:HEAD_V * (h + 1)].astype(MXU_DTYPE)
        vt_ref[0, 0, V_ROWS * h + HEAD_V:V_ROWS * (h + 1), :] = ones_row


def _proj_kernel(x_ref, ng_ref, w1_ref, w2t_ref, gq_ref, wuqm_ref, wuqs_ref, gkv_ref, wk_ref,
                 wvt_ref, cqt_ref, sqt_ref, ck_ref, sk_ref,
                 qat_ref, ka_ref, vat_ref, za_ref, qbt_ref, kb_ref, vbt_ref, zb_ref,
                 qit_ref, ki_ref, wit_ref, ga_ref, gb_ref):
    hb = _rms(x_ref[0], ng_ref[...]).astype(MXU_DTYPE)

    def cols(lo, hi):
        return _nn(hb, w1_ref[:, lo:hi])

    def rows(lo, hi):
        return _nt(w2t_ref[lo:hi, :], hb)

    qn = _rms(cols(0, 384), gq_ref[...]).astype(MXU_DTYPE)
    qm = _nt(wuqm_ref[...], qn)
    qs = _nt(wuqs_ref[...], qn)
    cqt = cqt_ref[...]
    sqt = sqt_ref[...]
    for h in range(MLA_HEADS):
        sl = slice(LANES * h, LANES * (h + 1))
        qat_ref[0, sl, :] = (qm[sl] * cqt + qs[sl] * sqt).astype(MXU_DTYPE)

    kvn = _rms(cols(384, 640), gkv_ref[...]).astype(MXU_DTYPE)
    kn = _nn(kvn, wk_ref[...])
    kpe = cols(640, 768) * ck_ref[...] + cols(768, 896) * sk_ref[...]
    for h in range(MLA_HEADS):
        sl = slice(LANES * h, LANES * (h + 1))
        ka_ref[0, :, sl] = (kn[:, sl] + kpe).astype(MXU_DTYPE)
    _store_values(vat_ref, _nt(wvt_ref[...], kvn))

    za = cols(896, 1408)
    za_ref[0] = (za * _sigmoid(za)).astype(MXU_DTYPE)
    kb_ref[0] = cols(1408, 1920).astype(MXU_DTYPE)
    zb = cols(1920, 2432)
    zb_ref[0] = (zb * _sigmoid(zb)).astype(MXU_DTYPE)
    ki_ref[0] = cols(2432, 2560).astype(MXU_DTYPE)
    ga_ref[0] = _sigmoid(cols(2560, 3584)).astype(MXU_DTYPE)
    gb_ref[0] = _sigmoid(cols(3584, 4608)).astype(MXU_DTYPE)

    qbt_ref[0] = (rows(0, 512) * (DSA_HEAD_DIM ** -0.5 * LOG2E)).astype(MXU_DTYPE)
    _store_values(vbt_ref, rows(512, 1024))
    qit_ref[0] = rows(1024, 1280).astype(MXU_DTYPE)
    wit_ref[0] = rows(1280, 1296)[:IDX_HEADS] * ((IDX_DIM * IDX_HEADS) ** -0.5)


def _diag_admissible():
    key = lax.broadcasted_iota(jnp.int32, (TILE, TILE), 0)
    qry = lax.broadcasted_iota(jnp.int32, (TILE, TILE), 1)
    return (key // CHUNK) <= (qry // CHUNK)


def _flash_init(m_ref, acc_ref):
    m_ref[...] = jnp.full(m_ref.shape, M_INIT, F32)
    acc_ref[...] = jnp.zeros(acc_ref.shape, F32)


class _Flash:
    def __init__(self, n, score, value, m_ref, mb_ref, acc_ref, s_refs, prepare=None):
        self.n, self.score, self.value, self.prepare = n, score, value, prepare
        self.m_ref, self.mb_ref, self.acc_ref, self.s_refs = m_ref, mb_ref, acc_ref, s_refs

    def put_scores(self, h, slot, s):
        self.s_refs[h][slot] = s
        mx = s[0:ROW_CHUNK]
        for c in range(1, TILE // ROW_CHUNK):
            mx = jnp.maximum(mx, s[ROW_CHUNK * c:ROW_CHUNK * (c + 1)])
        self.mb_ref[h, slot] = jnp.max(mx, axis=0, keepdims=True)

    def scores(self, j, slot, score=None):
        if self.prepare is not None:
            self.prepare(j, slot)
        for h in range(self.n):
            self.put_scores(h, slot, (score or self.score)(h, j, slot))

    def softmax_values(self, j, slot, nxt=None):
        rows = ROW_CHUNK
        n_chunks = TILE // rows
        if nxt is not None and self.prepare is not None:
            self.prepare(*nxt)
        for h in range(self.n):
            if nxt is not None:
                self.put_scores(h, nxt[1], self.score(h, *nxt))
            s_ref = self.s_refs[h]
            m_old = self.m_ref[h]
            m_new = jnp.maximum(m_old, self.mb_ref[h, slot])
            p = jnp.concatenate(
                [jnp.exp2(s_ref[slot, rows * c:rows * (c + 1)] - m_new).astype(MXU_DTYPE)
                 for c in range(n_chunks)], axis=0)
            self.m_ref[h] = m_new
            alpha = jnp.exp2(m_old - m_new)
            self.acc_ref[h] = alpha * self.acc_ref[h] + _nn(self.value(h, j), p)

    def block(self, j, score):
        self.scores(j, 0, score)
        self.softmax_values(j, 0)

    def pipeline(self, n_blocks):
        _pipeline(n_blocks, lambda: self.scores(0, 0),
                  lambda j, slot: self.softmax_values(j, slot, nxt=(j + 1, 1 - slot)),
                  self.softmax_values)


def _pipeline(n_blocks, first, step, last):
    @pl.when(n_blocks > 0)
    def _():
        first()
        n_pairs = (n_blocks - 1) // 2

        def body(t, c):
            step(2 * t, 0)
            step(2 * t + 1, 1)
            return c

        lax.fori_loop(0, n_pairs, body, 0)
        j = 2 * n_pairs

        @pl.when(j + 1 < n_blocks)
        def _():
            step(j, 0)
            last(j + 1, 1)

        @pl.when(j + 1 == n_blocks)
        def _():
            last(j, 0)


def _flash_finish(n_heads, z_ref, o_ref, acc_ref, y_ref):
    for h in range(n_heads):
        y_ref[HEAD_V * h:HEAD_V * (h + 1), :] = acc_ref[h, 0:HEAD_V] / acc_ref[h, HEAD_V:HEAD_V + 1]
    o_ref[0] = (y_ref[...].T * z_ref[0].astype(F32)).astype(MXU_DTYPE)


def _mla_kernel(qt_ref, k_ref, vt_ref, z_ref, o_ref, m_ref, mb_ref, acc_ref, y_ref, *s_refs):
    i = pl.program_id(1)
    _flash_init(m_ref, acc_ref)

    def score(h, j, slot):
        sl = slice(LANES * h, LANES * (h + 1))
        off = pl.multiple_of(j * TILE, TILE)
        return _nn(k_ref[0, pl.ds(off, TILE), sl], qt_ref[0, sl, :])

    def diag_score(h, j, slot):
        return jnp.where(_diag_admissible(), score(h, j, slot), NEG_INF)

    def value(h, j):
        return vt_ref[0, j, V_ROWS * h:V_ROWS * (h + 1), :]

    flash = _Flash(MLA_HEADS, score, value, m_ref, mb_ref, acc_ref, s_refs)
    flash.block(i, diag_score)
    flash.pipeline(i)
    _flash_finish(MLA_HEADS, z_ref, o_ref, acc_ref, y_ref)


def _float_key(v):
    bits = lax.bitcast_convert_type(v, jnp.int32)
    return bits ^ ((bits >> 31) & 0x7FFFFFFF)


def _key_float(k):
    return lax.bitcast_convert_type(k ^ ((k >> 31) & 0x7FFFFFFF), F32)


def _top_half(v):
    bits = lax.bitcast_convert_type(v, jnp.int32) & -KEY_GRID
    return lax.bitcast_convert_type(bits, F32)


def _any(mask):
    return jnp.max(jnp.where(mask, 1.0, 0.0)) > 0.5


def _dsa_kernel(topk, qbt_ref, kb_ref, vbt_ref, qit_ref, ki_ref, wit_ref, zb_ref, bias_ref, o_ref,
                sc_ref, sc16_ref, qm_ref, qim_ref, sel_ref, lohi_ref, m_ref, mb_ref, acc_ref, y_ref,
                *s_refs):
    i = pl.program_id(1)
    adm = _diag_admissible()
    n_chunks = TILE // ROW_CHUNK

    row = lax.broadcasted_iota(jnp.int32, (LANES, TILE), 0)
    for h in range(DSA_HEADS):
        g = h // 2
        qm_ref[h] = jnp.where((row // DSA_HEAD_DIM) == (h % 2),
                              qbt_ref[0, LANES * g:LANES * (g + 1), :], 0).astype(MXU_DTYPE)
        gi = h // 4
        qim_ref[h] = jnp.where((row // IDX_DIM) == (h % 4),
                               qit_ref[0, LANES * gi:LANES * (gi + 1), :], 0).astype(MXU_DTYPE)

    def fold8(op, acc, v):
        for r in range(v.shape[0] // 8):
            acc = op(acc, v[8 * r:8 * (r + 1)])
        return acc

    heads_per_chunk = IDX_HEADS // n_chunks

    def stage_dots(j, slot, heads):
        off = pl.multiple_of(j * TILE, TILE)
        kij = ki_ref[0, pl.ds(off, TILE), :]
        for h in heads:
            s_refs[h][slot] = _nn(kij, qim_ref[h])

    def fold_scores(j, slot, masked=False, nxt=None):
        off = pl.multiple_of(j * TILE, TILE)
        lo8, hi8, pos8, nonneg8 = (lohi_ref[k] for k in range(4))
        for c in range(n_chunks):
            if nxt is not None:
                stage_dots(*nxt, range(heads_per_chunk * c, heads_per_chunk * (c + 1)))
            rs = slice(ROW_CHUNK * c, ROW_CHUNK * (c + 1))
            tot = jnp.zeros((ROW_CHUNK, TILE), F32)
            for h in range(IDX_HEADS):
                tot = tot + jnp.maximum(s_refs[h][slot, rs], 0.0) * wit_ref[0, h:h + 1, :]
            if masked:
                lo8 = fold8(jnp.minimum, lo8, jnp.where(adm[rs], tot, -NEG_INF))
                tot = jnp.where(adm[rs], tot, NEG_INF)
            else:
                lo8 = fold8(jnp.minimum, lo8, tot)
            hi8 = fold8(jnp.maximum, hi8, tot)
            pos8 = fold8(jnp.add, pos8, jnp.where(tot > 0.0, 1.0, 0.0))
            nonneg8 = fold8(jnp.add, nonneg8, jnp.where(tot >= 0.0, 1.0, 0.0))
            rows = pl.ds(pl.multiple_of(off + ROW_CHUNK * c, ROW_CHUNK), ROW_CHUNK)
            sc_ref[rows, :] = tot
            sc16_ref[rows, :] = _top_half(tot).astype(jnp.bfloat16)
        for k, v in enumerate((lo8, hi8, pos8, nonneg8)):
            lohi_ref[k] = v

    lohi_ref[0] = jnp.full((8, TILE), -NEG_INF, F32)
    lohi_ref[1] = jnp.full((8, TILE), NEG_INF, F32)
    lohi_ref[2] = jnp.zeros((8, TILE), F32)
    lohi_ref[3] = jnp.zeros((8, TILE), F32)
    stage_dots(i, 0, range(IDX_HEADS))
    fold_scores(i, 0, masked=True)
    _pipeline(i, lambda: stage_dots(0, 0, range(IDX_HEADS)),
              lambda j, slot: fold_scores(j, slot, nxt=(j + 1, 1 - slot)),
              fold_scores)
    smin = jnp.min(lohi_ref[0], axis=0, keepdims=True)
    smax = jnp.max(lohi_ref[1], axis=0, keepdims=True)

    def count(pred):
        def body(jb, acc):
            for c in range(n_chunks):
                off = pl.multiple_of(jb * TILE + c * ROW_CHUNK, ROW_CHUNK)
                acc = acc + jnp.where(pred(sc_ref[pl.ds(off, ROW_CHUNK), :], off), 1.0, 0.0)
            return acc
        acc = lax.fori_loop(0, i + 1, body, jnp.zeros((ROW_CHUNK, TILE), F32))
        return jnp.sum(acc, axis=0, keepdims=True)

    def mid_of(lo, hi):
        return (lo >> 1) + (hi >> 1) + (lo & hi & 1)

    kf = float(topk)
    qpos = i * TILE + lax.broadcasted_iota(jnp.int32, (1, TILE), 1)
    n_adm = ((qpos // CHUNK + 1) * CHUNK).astype(F32)
    few = n_adm <= kf

    def count16(thr16):
        one, zero = jnp.ones((), jnp.bfloat16), jnp.zeros((), jnp.bfloat16)

        def body(jb, acc):
            for c in range(n_chunks):
                off = pl.multiple_of(jb * TILE + c * ROW_CHUNK, ROW_CHUNK)
                acc = acc + jnp.where(sc16_ref[pl.ds(off, ROW_CHUNK), :] >= thr16, one, zero)
            return acc
        acc = lax.fori_loop(0, i + 1, body, jnp.zeros((ROW_CHUNK, TILE), jnp.bfloat16))
        return jnp.sum(acc.astype(F32), axis=0, keepdims=True)

    def probe_of(lo, hi):
        margin = (hi >> 3) - (lo >> 3)
        probe = _float_key(0.5 * _key_float(lo) + 0.5 * _key_float(hi))
        probe = jnp.clip(probe, lo + margin, hi - margin)
        return jnp.clip(probe, lo + 1, hi - 1)

    def grid_probe(lo, hi):
        near = (probe_of(lo, hi) + (KEY_GRID // 2)) & -KEY_GRID
        above_lo = ((lo >> 16) + 1) << 16
        probe = jnp.where(jnp.logical_and(near > lo, near < hi), near, above_lo)
        return probe, jnp.logical_and(mid_of(lo, hi) != lo, probe < hi)

    def update(state, probe, c, active):
        lo, hi, cnt = state
        ge = c >= kf
        take_lo = jnp.logical_and(active, ge)
        new_hi = jnp.where(jnp.logical_and(active, jnp.logical_not(ge)), probe, hi)
        new_hi = jnp.where(jnp.logical_and(active, c == kf), probe + 1, new_hi)
        return jnp.where(take_lo, probe, lo), new_hi, jnp.where(take_lo, c, cnt)

    def coarse_pass(state):
        probe, active = grid_probe(state[0], state[1])
        thr16 = _top_half(_key_float(probe)).astype(jnp.bfloat16)
        return update(state, probe, count16(thr16), active)

    def fine_pass(state):
        lo, hi, _ = state
        probe = probe_of(lo, hi)
        thr = _key_float(probe)
        return update(state, probe, count(lambda blk, off: blk >= thr), mid_of(lo, hi) != lo)

    def search(one_pass, is_active, state):
        def cond(c):
            return jnp.logical_and(c[0] < 128, _any(is_active(c[1], c[2])))

        def body(c):
            return (c[0] + 2,) + one_pass(one_pass(c[1:]))

        return lax.while_loop(cond, body, (jnp.int32(0),) + state)[1:]

    c_pos = jnp.sum(lohi_ref[2], axis=0, keepdims=True)
    c_nonneg = jnp.sum(lohi_ref[3], axis=0, keepdims=True)
    at_zero = jnp.logical_and(c_pos < kf, kf <= c_nonneg)
    above = kf <= c_pos
    lo0 = jnp.where(at_zero, 0, jnp.where(above, 1, _float_key(smin)))
    hi0 = jnp.where(at_zero, 1, jnp.where(above, _float_key(smax) + 1, -1))
    cnt0 = jnp.where(at_zero, c_nonneg, jnp.where(above, c_pos, n_adm))
    lo0 = jnp.where(few, KEY_LO, lo0)
    hi0 = jnp.where(few, KEY_LO + 1, hi0)
    cnt0 = jnp.where(few, n_adm, cnt0)
    state = search(coarse_pass, lambda lo, hi: grid_probe(lo, hi)[1], (lo0, hi0, cnt0))
    lo, _, cnt = search(fine_pass, lambda lo, hi: mid_of(lo, hi) != lo, state)
    thr = _key_float(lo)

    excess = cnt > kf

    @pl.when(_any(excess))
    def _():
        need = kf - count(lambda blk, off: blk > thr)

        def tie_le(jmax):
            def pred(blk, off):
                idx = off + lax.broadcasted_iota(jnp.int32, (ROW_CHUNK, TILE), 0)
                return jnp.logical_and(blk == thr, idx <= jmax)
            return count(pred)

        def j_body(_, c):
            jlo, jhi = c
            jmid = (jlo + jhi) >> 1
            ok = tie_le(jmid) >= need
            return jnp.where(ok, jlo, jmid), jnp.where(ok, jmid, jhi)

        n_keys = (i + 1) * TILE
        j0 = (jnp.full((1, TILE), -1, jnp.int32), jnp.zeros((1, TILE), jnp.int32) + (n_keys - 1))
        _, jcut = lax.fori_loop(0, 14, j_body, j0)

        def drop(c, carry):
            off = pl.multiple_of(c * ROW_CHUNK, ROW_CHUNK)
            blk = sc_ref[pl.ds(off, ROW_CHUNK), :]
            idx = off + lax.broadcasted_iota(jnp.int32, (ROW_CHUNK, TILE), 0)
            kill = jnp.logical_and(jnp.logical_and(blk == thr, idx > jcut), excess)
            sc_ref[pl.ds(off, ROW_CHUNK), :] = jnp.where(kill, NEG_INF, blk)
            return carry

        lax.fori_loop(0, (i + 1) * n_chunks, drop, 0)

    _flash_init(m_ref, acc_ref)

    def select(j, slot):
        off = pl.multiple_of(j * TILE, TILE)
        sel_ref[slot] = jnp.where(sc_ref[pl.ds(off, TILE), :] >= thr, 0.0, NEG_INF)

    def score(h, j, slot, near=None):
        g = h // 2
        off = pl.multiple_of(j * TILE, TILE)
        s = _nn(kb_ref[0, pl.ds(off, TILE), LANES * g:LANES * (g + 1)], qm_ref[h])
        if near is not None:
            s = s + bias_ref[h, near]
        return s + sel_ref[slot]

    def value(h, j):
        return vbt_ref[0, j, V_ROWS * h:V_ROWS * (h + 1), :]

    flash = _Flash(DSA_HEADS, score, value, m_ref, mb_ref, acc_ref, s_refs, prepare=select)
    flash.block(i, functools.partial(score, near=1))

    @pl.when(i >= 1)
    def _():
        flash.block(i - 1, functools.partial(score, near=0))

    flash.pipeline(i - 1)
    _flash_finish(DSA_HEADS, zb_ref, o_ref, acc_ref, y_ref)


def _merge_kernel(x_ref, ya_ref, yb_ref, ga_ref, gb_ref, woa_ref, wob_ref, wout_ref, fg_ref, o_ref):
    merged = (ga_ref[0].astype(F32) * _nn(ya_ref[0], woa_ref[...])
              + gb_ref[0].astype(F32) * _nn(yb_ref[0], wob_ref[...]))
    y = x_ref[0] + _nn(merged.astype(MXU_DTYPE), wout_ref[...])
    o_ref[0] = _rms(y, fg_ref[...])


def _t5_bucket(rel):
    nb = REL_BUCKETS // 2
    max_exact = nb // 2
    ret = (rel > 0).astype(jnp.int32) * nb
    n = jnp.abs(rel)
    nf = jnp.maximum(n, 1).astype(jnp.float32)
    large = max_exact + (jnp.log(nf / max_exact) / np.log(REL_MAX_DIST / max_exact)
                         * (nb - max_exact)).astype(jnp.int32)
    large = jnp.minimum(large, nb - 1)
    return ret + jnp.where(n < max_exact, n, large)


def _bias_tables(rel_bias):
    key = jnp.arange(TILE, dtype=jnp.int32)[:, None]
    qry = jnp.arange(TILE, dtype=jnp.int32)[None, :]
    rel_diag = key - qry
    far = rel_bias[_t5_bucket(jnp.int32(-(TILE + 1)))]

    def table(rel):
        bucket = _t5_bucket(rel)[None]
        out = jnp.zeros((DSA_HEADS, TILE, TILE), F32)
        for b in range(REL_BUCKETS):
            out = jnp.where(bucket == b, rel_bias[b][:, None, None], out)
        return (out - far[:, None, None]) * LOG2E

    diag = jnp.where(((key // CHUNK) <= (qry // CHUNK))[None], table(rel_diag), NEG_INF)
    return jnp.stack([table(rel_diag - TILE), diag], axis=1).astype(F32)


def _rope_tables(seq):
    half = MLA_ROPE // 2
    freqs = ROPE_BASE ** (-jnp.arange(half, dtype=jnp.float32) / half)
    ang = jnp.arange(seq, dtype=jnp.int32).astype(jnp.float32)[:, None] * freqs[None, :]
    cos, sin = jnp.cos(ang), jnp.sin(ang)
    zeros = jnp.zeros((seq, LANES - MLA_NOPE - MLA_ROPE), F32)
    lead = jnp.zeros((seq, MLA_NOPE), F32)
    c_k = jnp.concatenate([lead, cos, cos, zeros], axis=1)
    s_k = jnp.concatenate([lead, -sin, sin, zeros], axis=1)
    scale = (MLA_NOPE + MLA_ROPE) ** -0.5 * LOG2E
    c_q = jnp.concatenate([lead + 1.0, cos, cos, zeros], axis=1) * scale
    return c_q.T, (s_k * scale).T, c_k, s_k


def _prep_weights(w_in, w_uq, w_ukv):
    cuts = np.cumsum(IN_SPLITS)[:-1].tolist()
    (w_qlat, w_ckv, w_kr, w_za, w_qb, w_kb, w_vb, w_zb,
     w_qi, w_ki, w_wi, w_ga, w_gb) = jnp.split(w_in, cuts, axis=1)
    half = MLA_ROPE // 2
    swap = np.concatenate([np.arange(half, MLA_ROPE), np.arange(half)])

    def z(n):
        return jnp.zeros((D_MODEL, n), F32)

    pad = LANES - MLA_NOPE - MLA_ROPE
    kr = jnp.concatenate([z(MLA_NOPE), w_kr, z(pad)], axis=1)
    krs = jnp.concatenate([z(MLA_NOPE), w_kr[:, swap], z(pad)], axis=1)
    ki4 = jnp.tile(w_ki, (1, LANES // IDX_DIM))
    w1 = jnp.concatenate([w_qlat, w_ckv, kr, krs, w_za, w_kb, w_zb, ki4, w_ga, w_gb], axis=1)
    w2t = jnp.concatenate([w_qb, w_vb, w_qi, w_wi, z(BF16_ROWS - IDX_HEADS)], axis=1).T

    wuq = w_uq.reshape(MLA_Q_RANK, MLA_HEADS, MLA_NOPE + MLA_ROPE)
    zq = jnp.zeros((MLA_Q_RANK, MLA_HEADS, pad), F32)
    wuqm = jnp.concatenate([wuq, zq], axis=2).reshape(MLA_Q_RANK, MLA_HEADS * LANES)
    wuqs = jnp.concatenate([jnp.zeros((MLA_Q_RANK, MLA_HEADS, MLA_NOPE), F32),
                            wuq[:, :, MLA_NOPE:][:, :, swap], zq], axis=2
                           ).reshape(MLA_Q_RANK, MLA_HEADS * LANES)
    wukv = w_ukv.reshape(MLA_KV_RANK, MLA_HEADS, MLA_NOPE + MLA_V)
    wk = jnp.concatenate([wukv[:, :, :MLA_NOPE],
                          jnp.zeros((MLA_KV_RANK, MLA_HEADS, LANES - MLA_NOPE), F32)], axis=2
                         ).reshape(MLA_KV_RANK, MLA_HEADS * LANES)
    wv = wukv[:, :, MLA_NOPE:].reshape(MLA_KV_RANK, MLA_WIDTH)
    c = lambda a: a.astype(MXU_DTYPE)
    return c(w1), c(w2t), c(wuqm.T), c(wuqs.T), c(wk), c(wv.T)


def _full(shape):
    return pl.BlockSpec(shape, lambda *_: (0,) * len(shape))


def _resident(shape, index_map):
    return pl.BlockSpec(shape, index_map, pipeline_mode=pl.Buffered(1))


def kernel(x, norm_g, w_in, g_q_lat, w_uq, g_kv_lat, w_ukv, w_o_a, w_o_b, w_out, rel_bias, final_g):
    B, S, D = x.shape
    assert D == D_MODEL and S % TILE == 0 and norm_g.shape[0] == 1
    assert MLA_V == HEAD_V and DSA_HEAD_DIM == HEAD_V
    assert S // ROW_CHUNK <= 256
    nb = S // TILE
    topk = min(TOPK_MAX, S // 4)
    bf = MXU_DTYPE
    params = functools.partial(pltpu.CompilerParams, vmem_limit_bytes=VMEM_LIMIT)

    w1, w2t, wuqm_t, wuqs_t, wk, wv_t = _prep_weights(w_in[0], w_uq[0], w_ukv[0])
    cq_t, sq_t, c_k, s_k = _rope_tables(S)
    n1, n2 = w1.shape[1], w2t.shape[0]
    va_rows, vb_rows = MLA_HEADS * V_ROWS, DSA_HEADS * V_ROWS

    tok = lambda c: pl.BlockSpec((1, TILE, c), lambda b, r: (b, r, 0))
    tok_t = lambda c: pl.BlockSpec((1, c, TILE), lambda b, r: (b, 0, r))
    blk_t = lambda c: pl.BlockSpec((1, 1, c, TILE), lambda b, r: (b, r, 0, 0))
    sds = jax.ShapeDtypeStruct
    (qa_t, ka, va_t, za, qb_t, kb, vb_t, zb, qi_t, ki, wi_t, ga, gb) = pl.pallas_call(
        _proj_kernel,
        grid=(B, nb),
        in_specs=[tok(D), _full((1, D)), _full((D, n1)), _full((n2, D)),
                  _full((1, MLA_Q_RANK)), _full((MLA_HEADS * LANES, MLA_Q_RANK)),
                  _full((MLA_HEADS * LANES, MLA_Q_RANK)),
                  _full((1, MLA_KV_RANK)), _full((MLA_KV_RANK, MLA_HEADS * LANES)),
                  _full((MLA_WIDTH, MLA_KV_RANK)),
                  pl.BlockSpec((LANES, TILE), lambda b, r: (0, r)),
                  pl.BlockSpec((LANES, TILE), lambda b, r: (0, r)),
                  pl.BlockSpec((TILE, LANES), lambda b, r: (r, 0)),
                  pl.BlockSpec((TILE, LANES), lambda b, r: (r, 0))],
        out_specs=[tok_t(MLA_HEADS * LANES), tok(MLA_HEADS * LANES), blk_t(va_rows), tok(MLA_WIDTH),
                   tok_t(DSA_WIDTH), tok(DSA_WIDTH), blk_t(vb_rows), tok(DSA_WIDTH),
                   tok_t(IDX_HEADS * IDX_DIM), tok(LANES), tok_t(IDX_HEADS), tok(D), tok(D)],
        out_shape=[sds((B, MLA_HEADS * LANES, S), bf), sds((B, S, MLA_HEADS * LANES), bf),
                   sds((B, nb, va_rows, TILE), bf), sds((B, S, MLA_WIDTH), bf),
                   sds((B, DSA_WIDTH, S), bf), sds((B, S, DSA_WIDTH), bf),
                   sds((B, nb, vb_rows, TILE), bf), sds((B, S, DSA_WIDTH), bf),
                   sds((B, IDX_HEADS * IDX_DIM, S), bf), sds((B, S, LANES), bf),
                   sds((B, IDX_HEADS, S), F32), sds((B, S, D), bf), sds((B, S, D), bf)],
        compiler_params=params(dimension_semantics=("parallel", "parallel")),
        name="proj",
    )(x, norm_g, w1, w2t, g_q_lat, wuqm_t, wuqs_t, g_kv_lat, wk, wv_t, cq_t, sq_t, c_k, s_k)

    def flash_state(heads):
        return ([pltpu.VMEM((heads, 1, TILE), F32),
                 pltpu.VMEM((heads, 2, 1, TILE), F32),
                 pltpu.VMEM((heads, V_ROWS, TILE), F32),
                 pltpu.VMEM((heads * HEAD_V, TILE), F32)]
                + [pltpu.VMEM((2, TILE, TILE), F32)] * heads)

    ya = pl.pallas_call(
        _mla_kernel,
        grid=(B, nb),
        in_specs=[pl.BlockSpec((1, MLA_HEADS * LANES, TILE), lambda b, i: (b, 0, i)),
                  _resident((1, S, MLA_HEADS * LANES), lambda b, i: (b, 0, 0)),
                  _resident((1, nb, va_rows, TILE), lambda b, i: (b, 0, 0, 0)),
                  pl.BlockSpec((1, TILE, MLA_WIDTH), lambda b, i: (b, i, 0))],
        out_specs=pl.BlockSpec((1, TILE, MLA_WIDTH), lambda b, i: (b, i, 0)),
        out_shape=sds((B, S, MLA_WIDTH), bf),
        scratch_shapes=flash_state(MLA_HEADS),
        compiler_params=params(dimension_semantics=("parallel", "arbitrary")),
        name="mla",
    )(qa_t, ka, va_t, za)

    yb = pl.pallas_call(
        functools.partial(_dsa_kernel, topk),
        grid=(B, nb),
        in_specs=[pl.BlockSpec((1, DSA_WIDTH, TILE), lambda b, i: (b, 0, i)),
                  _resident((1, S, DSA_WIDTH), lambda b, i: (b, 0, 0)),
                  _resident((1, nb, vb_rows, TILE), lambda b, i: (b, 0, 0, 0)),
                  pl.BlockSpec((1, IDX_HEADS * IDX_DIM, TILE), lambda b, i: (b, 0, i)),
                  _resident((1, S, LANES), lambda b, i: (b, 0, 0)),
                  pl.BlockSpec((1, IDX_HEADS, TILE), lambda b, i: (b, 0, i)),
                  pl.BlockSpec((1, TILE, DSA_WIDTH), lambda b, i: (b, i, 0)),
                  _resident((DSA_HEADS, 2, TILE, TILE), lambda b, i: (0, 0, 0, 0))],
        out_specs=pl.BlockSpec((1, TILE, DSA_WIDTH), lambda b, i: (b, i, 0)),
        out_shape=sds((B, S, DSA_WIDTH), bf),
        scratch_shapes=[pltpu.VMEM((S, TILE), F32),
                        pltpu.VMEM((S, TILE), jnp.bfloat16),
                        pltpu.VMEM((DSA_HEADS, LANES, TILE), bf),
                        pltpu.VMEM((IDX_HEADS, LANES, TILE), bf),
                        pltpu.VMEM((2, TILE, TILE), F32),
                        pltpu.VMEM((4, 8, TILE), F32),
                        *flash_state(DSA_HEADS)],
        compiler_params=params(dimension_semantics=("parallel", "arbitrary")),
        name="dsa",
    )(qb_t, kb, vb_t, qi_t, ki, wi_t, zb, _bias_tables(rel_bias))

    return pl.pallas_call(
        _merge_kernel,
        grid=(B, nb),
        in_specs=[tok(D), tok(MLA_WIDTH), tok(DSA_WIDTH), tok(D), tok(D),
                  _full((MLA_WIDTH, D)), _full((DSA_WIDTH, D)), _full((D, D)), _full((1, D))],
        out_specs=tok(D),
        out_shape=sds((B, S, D), x.dtype),
        compiler_params=params(dimension_semantics=("parallel", "parallel")),
        name="merge",
    )(x, ya, yb, ga, gb, w_o_a[0].astype(bf), w_o_b[0].astype(bf), w_out[0].astype(bf),
      final_g.reshape(1, D))
```

```python
import functools
import math

import numpy as np
import jax
import jax.numpy as jnp
from jax import lax
from jax.experimental import pallas as pl
from jax.experimental.pallas import tpu as pltpu

D_MODEL = 1024
CHUNK = 64
EPS = 1e-6
MLA_HEADS = 8
MLA_NOPE = 64
MLA_ROPE = 32
MLA_V = 64
MLA_Q_RANK = 384
MLA_KV_RANK = 256
ROPE_BASE = 10000.0
MLA_WIDTH = MLA_HEADS * MLA_V
DSA_HEADS = 8
DSA_HEAD_DIM = 64
DSA_WIDTH = DSA_HEADS * DSA_HEAD_DIM
IDX_HEADS = 8
IDX_DIM = 32
TOPK_MAX = 256
REL_BUCKETS = 32
REL_MAX_DIST = 128
IN_SPLITS = (MLA_Q_RANK, MLA_KV_RANK, MLA_ROPE, MLA_WIDTH,
             DSA_WIDTH, DSA_WIDTH, DSA_WIDTH, DSA_WIDTH,
             IDX_HEADS * IDX_DIM, IDX_DIM, IDX_HEADS,
             D_MODEL, D_MODEL)

LANES = 128
BF16_ROWS = 16
TILE = 256
ROW_CHUNK = 64
HEAD_V = 64
V_ROWS = HEAD_V + BF16_ROWS
VMEM_LIMIT = 56 * 1024 * 1024
MXU_DTYPE = jnp.bfloat16
F32 = jnp.float32
NEG_INF = float("-inf")
M_INIT = -1e30
LOG2E = math.log2(math.e)

KEY_LO = int(np.int32(np.uint32(0x80800000)))
KEY_GRID = 1 << 16


def _nt(a, b):
    return lax.dot_general(a, b, (((1,), (1,)), ((), ())), preferred_element_type=F32)


def _nn(a, b):
    return jnp.dot(a, b, preferred_element_type=F32)


def _sigmoid(v):
    return 1.0 / (1.0 + jnp.exp(-v))


def _rms(v, g):
    return v * lax.rsqrt(jnp.mean(v * v, axis=-1, keepdims=True) + EPS) * g


def _store_values(vt_ref, v_t):
    row = lax.broadcasted_iota(jnp.int32, (BF16_ROWS, TILE), 0)
    ones_row = jnp.where(row == 0, 1.0, 0.0).astype(MXU_DTYPE)
    for h in range(v_t.shape[0] // HEAD_V):
        vt_ref[0, 0, V_ROWS * h:V_ROWS * h + HEAD_V, :] = v_t[HEAD_V * h:HEAD_V * (h + 1)].astype(MXU_DTYPE)
        vt_ref[0, 0, V_ROWS * h + HEAD_V:V_ROWS * (h + 1), :] = ones_row


def _proj_kernel(x_ref, ng_ref, w1_ref, w2t_ref, gq_ref, wuqm_ref, wuqs_ref, gkv_ref, wk_ref,
                 wvt_ref, cqt_ref, sqt_ref, ck_ref, sk_ref,
                 qat_ref, ka_ref, vat_ref, za_ref, qbt_ref, kb_ref, vbt_ref, zb_ref,
                 qit_ref, ki_ref, wit_ref, ga_ref, gb_ref):
    hb = _rms(x_ref[0], ng_ref[...]).astype(MXU_DTYPE)

    def cols(lo, hi):
        return _nn(hb, w1_ref[:, lo:hi])

    def rows(lo, hi):
        return _nt(w2t_ref[lo:hi, :], hb)

    qn = _rms(cols(0, 384), gq_ref[...]).astype(MXU_DTYPE)
    qm = _nt(wuqm_ref[...], qn)
    qs = _nt(wuqs_ref[...], qn)
    cqt = cqt_ref[...]
    sqt = sqt_ref[...]
    for h in range(MLA_HEADS):
        sl = slice(LANES * h, LANES * (h + 1))
        qat_ref[0, sl, :] = (qm[sl] * cqt + qs[sl] * sqt).astype(MXU_DTYPE)

    kvn = _rms(cols(384, 640), gkv_ref[...]).astype(MXU_DTYPE)
    kn = _nn(kvn, wk_ref[...])
    kpe = cols(640, 768) * ck_ref[...] + cols(768, 896) * sk_ref[...]
    for h in range(MLA_HEADS):
        sl = slice(LANES * h, LANES * (h + 1))
        ka_ref[0, :, sl] = (kn[:, sl] + kpe).astype(MXU_DTYPE)
    _store_values(vat_ref, _nt(wvt_ref[...], kvn))

    za = cols(896, 1408)
    za_ref[0] = (za * _sigmoid(za)).astype(MXU_DTYPE)
    kb_ref[0] = cols(1408, 1920).astype(MXU_DTYPE)
    zb = cols(1920, 2432)
    zb_ref[0] = (zb * _sigmoid(zb)).astype(MXU_DTYPE)
    ki_ref[0] = cols(2432, 2560).astype(MXU_DTYPE)
    ga_ref[0] = _sigmoid(cols(2560, 3584)).astype(MXU_DTYPE)
    gb_ref[0] = _sigmoid(cols(3584, 4608)).astype(MXU_DTYPE)

    qbt_ref[0] = (rows(0, 512) * (DSA_HEAD_DIM ** -0.5 * LOG2E)).astype(MXU_DTYPE)
    _store_values(vbt_ref, rows(512, 1024))
    qit_ref[0] = rows(1024, 1280).astype(MXU_DTYPE)
    wit_ref[0] = rows(1280, 1296)[:IDX_HEADS] * ((IDX_DIM * IDX_HEADS) ** -0.5)


def _diag_admissible():
    key = lax.broadcasted_iota(jnp.int32, (TILE, TILE), 0)
    qry = lax.broadcasted_iota(jnp.int32, (TILE, TILE), 1)
    return (key // CHUNK) <= (qry // CHUNK)


def _flash_init(m_ref, acc_ref):
    m_ref[...] = jnp.full(m_ref.shape, M_INIT, F32)
    acc_ref[...] = jnp.zeros(acc_ref.shape, F32)


class _Flash:
    def __init__(self, n, score, value, m_ref, mb_ref, acc_ref, s_refs, prepare=None):
        self.n, self.score, self.value, self.prepare = n, score, value, prepare
        self.m_ref, self.mb_ref, self.acc_ref, self.s_refs = m_ref, mb_ref, acc_ref, s_refs

    def put_scores(self, h, slot, s):
        self.s_refs[h][slot] = s
        mx = s[0:ROW_CHUNK]
        for c in range(1, TILE // ROW_CHUNK):
            mx = jnp.maximum(mx, s[ROW_CHUNK * c:ROW_CHUNK * (c + 1)])
        self.mb_ref[h, slot] = jnp.max(mx, axis=0, keepdims=True)

    def scores(self, j, slot, score=None):
        if self.prepare is not None:
            self.prepare(j, slot)
        for h in range(self.n):
            self.put_scores(h, slot, (score or self.score)(h, j, slot))

    def softmax_values(self, j, slot, nxt=None):
        rows = ROW_CHUNK
        n_chunks = TILE // rows
        if nxt is not None and self.prepare is not None:
            self.prepare(*nxt)
        for h in range(self.n):
            if nxt is not None:
                self.put_scores(h, nxt[1], self.score(h, *nxt))
            s_ref = self.s_refs[h]
            m_old = self.m_ref[h]
            m_new = jnp.maximum(m_old, self.mb_ref[h, slot])
            p = jnp.concatenate(
                [jnp.exp2(s_ref[slot, rows * c:rows * (c + 1)] - m_new).astype(MXU_DTYPE)
                 for c in range(n_chunks)], axis=0)
            self.m_ref[h] = m_new
            alpha = jnp.exp2(m_old - m_new)
            self.acc_ref[h] = alpha * self.acc_ref[h] + _nn(self.value(h, j), p)

    def block(self, j, score):
        self.scores(j, 0, score)
        self.softmax_values(j, 0)

    def pipeline(self, n_blocks):
        _pipeline(n_blocks, lambda: self.scores(0, 0),
                  lambda j, slot: self.softmax_values(j, slot, nxt=(j + 1, 1 - slot)),
                  self.softmax_values)


def _pipeline(n_blocks, first, step, last):
    @pl.when(n_blocks > 0)
    def _():
        first()
        n_pairs = (n_blocks - 1) // 2

        def body(t, c):
            step(2 * t, 0)
            step(2 * t + 1, 1)
            return c

        lax.fori_loop(0, n_pairs, body, 0)
        j = 2 * n_pairs

        @pl.when(j + 1 < n_blocks)
        def _():
            step(j, 0)
            last(j + 1, 1)

        @pl.when(j + 1 == n_blocks)
        def _():
            last(j, 0)


def _flash_finish(n_heads, z_ref, o_ref, acc_ref, y_ref):
    for h in range(n_heads):
        y_ref[HEAD_V * h:HEAD_V * (h + 1), :] = acc_ref[h, 0:HEAD_V] / acc_ref[h, HEAD_V:HEAD_V + 1]
    o_ref[0] = (y_ref[...].T * z_ref[0].astype(F32)).astype(MXU_DTYPE)


def _mla_kernel(qt_ref, k_ref, vt_ref, z_ref, o_ref, m_ref, mb_ref, acc_ref, y_ref, *s_refs):
    i = pl.program_id(1)
    _flash_init(m_ref, acc_ref)

    def score(h, j, slot):
        sl = slice(LANES * h, LANES * (h + 1))
        off = pl.multiple_of(j * TILE, TILE)
        return _nn(k_ref[0, pl.ds(off, TILE), sl], qt_ref[0, sl, :])

    def diag_score(h, j, slot):
        return jnp.where(_diag_admissible(), score(h, j, slot), NEG_INF)

    def value(h, j):
        return vt_ref[0, j, V_ROWS * h:V_ROWS * (h + 1), :]

    flash = _Flash(MLA_HEADS, score, value, m_ref, mb_ref, acc_ref, s_refs)
    flash.block(i, diag_score)
    flash.pipeline(i)
    _flash_finish(MLA_HEADS, z_ref, o_ref, acc_ref, y_ref)


def _float_key(v):
    bits = lax.bitcast_convert_type(v, jnp.int32)
    return bits ^ ((bits >> 31) & 0x7FFFFFFF)


def _key_float(k):
    return lax.bitcast_convert_type(k ^ ((k >> 31) & 0x7FFFFFFF), F32)


def _top_half(v):
    bits = lax.bitcast_convert_type(v, jnp.int32) & -KEY_GRID
    return lax.bitcast_convert_type(bits, F32)


def _any(mask):
    return jnp.max(jnp.where(mask, 1.0, 0.0)) > 0.5


def _dsa_kernel(topk, qbt_ref, kb_ref, vbt_ref, qit_ref, ki_ref, wit_ref, zb_ref, bias_ref, o_ref,
                sc_ref, sc16_ref, qm_ref, qim_ref, sel_ref, lohi_ref, m_ref, mb_ref, acc_ref, y_ref,
                *s_refs):
    i = pl.program_id(1)
    adm = _diag_admissible()
    n_chunks = TILE // ROW_CHUNK

    row = lax.broadcasted_iota(jnp.int32, (LANES, TILE), 0)
    for h in range(DSA_HEADS):
        g = h // 2
        qm_ref[h] = jnp.where((row // DSA_HEAD_DIM) == (h % 2),
                              qbt_ref[0, LANES * g:LANES * (g + 1), :], 0).astype(MXU_DTYPE)
        gi = h // 4
        qim_ref[h] = jnp.where((row // IDX_DIM) == (h % 4),
                               qit_ref[0, LANES * gi:LANES * (gi + 1), :], 0).astype(MXU_DTYPE)

    def fold8(op, acc, v):
        for r in range(v.shape[0] // 8):
            acc = op(acc, v[8 * r:8 * (r + 1)])
        return acc

    heads_per_chunk = IDX_HEADS // n_chunks

    def stage_dots(j, slot, heads):
        off = pl.multiple_of(j * TILE, TILE)
        kij = ki_ref[0, pl.ds(off, TILE), :]
        for h in heads:
            s_refs[h][slot] = _nn(kij, qim_ref[h])

    def fold_scores(j, slot, masked=False, nxt=None):
        off = pl.multiple_of(j * TILE, TILE)
        lo8, hi8, pos8, nonneg8 = (lohi_ref[k] for k in range(4))
        for c in range(n_chunks):
            if nxt is not None:
                stage_dots(*nxt, range(heads_per_chunk * c, heads_per_chunk * (c + 1)))
            rs = slice(ROW_CHUNK * c, ROW_CHUNK * (c + 1))
            tot = jnp.zeros((ROW_CHUNK, TILE), F32)
            for h in range(IDX_HEADS):
                tot = tot + jnp.maximum(s_refs[h][slot, rs], 0.0) * wit_ref[0, h:h + 1, :]
            if masked:
                lo8 = fold8(jnp.minimum, lo8, jnp.where(adm[rs], tot, -NEG_INF))
                tot = jnp.where(adm[rs], tot, NEG_INF)
            else:
                lo8 = fold8(jnp.minimum, lo8, tot)
            hi8 = fold8(jnp.maximum, hi8, tot)
            pos8 = fold8(jnp.add, pos8, jnp.where(tot > 0.0, 1.0, 0.0))
            nonneg8 = fold8(jnp.add, nonneg8, jnp.where(tot >= 0.0, 1.0, 0.0))
            rows = pl.ds(pl.multiple_of(off + ROW_CHUNK * c, ROW_CHUNK), ROW_CHUNK)
            sc_ref[rows, :] = tot
            sc16_ref[rows, :] = _top_half(tot).astype(jnp.bfloat16)
        for k, v in enumerate((lo8, hi8, pos8, nonneg8)):
            lohi_ref[k] = v

    lohi_ref[0] = jnp.full((8, TILE), -NEG_INF, F32)
    lohi_ref[1] = jnp.full((8, TILE), NEG_INF, F32)
    lohi_ref[2] = jnp.zeros((8, TILE), F32)
    lohi_ref[3] = jnp.zeros((8, TILE), F32)
    stage_dots(i, 0, range(IDX_HEADS))
    fold_scores(i, 0, masked=True)
    _pipeline(i, lambda: stage_dots(0, 0, range(IDX_HEADS)),
              lambda j, slot: fold_scores(j, slot, nxt=(j + 1, 1 - slot)),
              fold_scores)
    smin = jnp.min(lohi_ref[0], axis=0, keepdims=True)
    smax = jnp.max(lohi_ref[1], axis=0, keepdims=True)

    def count(pred):
        def body(jb, acc):
            for c in range(n_chunks):
                off = pl.multiple_of(jb * TILE + c * ROW_CHUNK, ROW_CHUNK)
                acc = acc + jnp.where(pred(sc_ref[pl.ds(off, ROW_CHUNK), :], off), 1.0, 0.0)
            return acc
        acc = lax.fori_loop(0, i + 1, body, jnp.zeros((ROW_CHUNK, TILE), F32))
        return jnp.sum(acc, axis=0, keepdims=True)

    def mid_of(lo, hi):
        return (lo >> 1) + (hi >> 1) + (lo & hi & 1)

    kf = float(topk)
    qpos = i * TILE + lax.broadcasted_iota(jnp.int32, (1, TILE), 1)
    n_adm = ((qpos // CHUNK + 1) * CHUNK).astype(F32)
    few = n_adm <= kf

    def count16(thr16):
        one, zero = jnp.ones((), jnp.bfloat16), jnp.zeros((), jnp.bfloat16)

        def body(jb, acc):
            for c in range(n_chunks):
                off = pl.multiple_of(jb * TILE + c * ROW_CHUNK, ROW_CHUNK)
                acc = acc + jnp.where(sc16_ref[pl.ds(off, ROW_CHUNK), :] >= thr16, one, zero)
            return acc
        acc = lax.fori_loop(0, i + 1, body, jnp.zeros((ROW_CHUNK, TILE), jnp.bfloat16))
        return jnp.sum(acc.astype(F32), axis=0, keepdims=True)

    def probe_of(lo, hi):
        margin = (hi >> 3) - (lo >> 3)
        probe = _float_key(0.5 * _key_float(lo) + 0.5 * _key_float(hi))
        probe = jnp.clip(probe, lo + margin, hi - margin)
        return jnp.clip(probe, lo + 1, hi - 1)

    def grid_probe(lo, hi):
        near = (probe_of(lo, hi) + (KEY_GRID // 2)) & -KEY_GRID
        above_lo = ((lo >> 16) + 1) << 16
        probe = jnp.where(jnp.logical_and(near > lo, near < hi), near, above_lo)
        return probe, jnp.logical_and(mid_of(lo, hi) != lo, probe < hi)

    def update(state, probe, c, active):
        lo, hi, cnt = state
        ge = c >= kf
        take_lo = jnp.logical_and(active, ge)
        new_hi = jnp.where(jnp.logical_and(active, jnp.logical_not(ge)), probe, hi)
        new_hi = jnp.where(jnp.logical_and(active, c == kf), probe + 1, new_hi)
        return jnp.where(take_lo, probe, lo), new_hi, jnp.where(take_lo, c, cnt)

    def coarse_pass(state):
        probe, active = grid_probe(state[0], state[1])
        thr16 = _top_half(_key_float(probe)).astype(jnp.bfloat16)
        return update(state, probe, count16(thr16), active)

    def fine_pass(state):
        lo, hi, _ = state
        probe = probe_of(lo, hi)
        thr = _key_float(probe)
        return update(state, probe, count(lambda blk, off: blk >= thr), mid_of(lo, hi) != lo)

    def search(one_pass, is_active, state):
        def cond(c):
            return jnp.logical_and(c[0] < 128, _any(is_active(c[1], c[2])))

        def body(c):
            return (c[0] + 2,) + one_pass(one_pass(c[1:]))

        return lax.while_loop(cond, body, (jnp.int32(0),) + state)[1:]

    c_pos = jnp.sum(lohi_ref[2], axis=0, keepdims=True)
    c_nonneg = jnp.sum(lohi_ref[3], axis=0, keepdims=True)
    at_zero = jnp.logical_and(c_pos < kf, kf <= c_nonneg)
    above = kf <= c_pos
    lo0 = jnp.where(at_zero, 0, jnp.where(above, 1, _float_key(smin)))
    hi0 = jnp.where(at_zero, 1, jnp.where(above, _float_key(smax) + 1, -1))
    cnt0 = jnp.where(at_zero, c_nonneg, jnp.where(above, c_pos, n_adm))
    lo0 = jnp.where(few, KEY_LO, lo0)
    hi0 = jnp.where(few, KEY_LO + 1, hi0)
    cnt0 = jnp.where(few, n_adm, cnt0)
    state = search(coarse_pass, lambda lo, hi: grid_probe(lo, hi)[1], (lo0, hi0, cnt0))
    lo, _, cnt = search(fine_pass, lambda lo, hi: mid_of(lo, hi) != lo, state)
    thr = _key_float(lo)

    excess = cnt > kf

    @pl.when(_any(excess))
    def _():
        need = kf - count(lambda blk, off: blk > thr)
        key = lax.broadcasted_iota(jnp.int32, (TILE, TILE), 0)
        upto = lax.broadcasted_iota(jnp.int32, (TILE, TILE), 1)
        prefix = jnp.where(upto <= key, 1.0, 0.0).astype(MXU_DTYPE)

        def drop(jb, seen):
            off = pl.multiple_of(jb * TILE, TILE)
            blk = sc_ref[pl.ds(off, TILE), :]
            tie = blk == thr
            rank = seen + _nn(prefix, jnp.where(tie, 1.0, 0.0).astype(MXU_DTYPE))
            sc_ref[pl.ds(off, TILE), :] = jnp.where(jnp.logical_and(tie, rank > need), NEG_INF, blk)
            return rank[TILE - 1:TILE, :]

        lax.fori_loop(0, i + 1, drop, jnp.zeros((1, TILE), F32))

    _flash_init(m_ref, acc_ref)

    def select(j, slot):
        off = pl.multiple_of(j * TILE, TILE)
        sel_ref[slot] = jnp.where(sc_ref[pl.ds(off, TILE), :] >= thr, 0.0, NEG_INF)

    def score(h, j, slot, near=None):
        g = h // 2
        off = pl.multiple_of(j * TILE, TILE)
        s = _nn(kb_ref[0, pl.ds(off, TILE), LANES * g:LANES * (g + 1)], qm_ref[h])
        if near is not None:
            s = s + bias_ref[h, near]
        return s + sel_ref[slot]

    def value(h, j):
        return vbt_ref[0, j, V_ROWS * h:V_ROWS * (h + 1), :]

    flash = _Flash(DSA_HEADS, score, value, m_ref, mb_ref, acc_ref, s_refs, prepare=select)
    flash.block(i, functools.partial(score, near=1))

    @pl.when(i >= 1)
    def _():
        flash.block(i - 1, functools.partial(score, near=0))

    flash.pipeline(i - 1)
    _flash_finish(DSA_HEADS, zb_ref, o_ref, acc_ref, y_ref)


def _merge_kernel(x_ref, ya_ref, yb_ref, ga_ref, gb_ref, woa_ref, wob_ref, wout_ref, fg_ref, o_ref):
    merged = (ga_ref[0].astype(F32) * _nn(ya_ref[0], woa_ref[...])
              + gb_ref[0].astype(F32) * _nn(yb_ref[0], wob_ref[...]))
    y = x_ref[0] + _nn(merged.astype(MXU_DTYPE), wout_ref[...])
    o_ref[0] = _rms(y, fg_ref[...])


def _t5_bucket(rel):
    nb = REL_BUCKETS // 2
    max_exact = nb // 2
    ret = (rel > 0).astype(jnp.int32) * nb
    n = jnp.abs(rel)
    nf = jnp.maximum(n, 1).astype(jnp.float32)
    large = max_exact + (jnp.log(nf / max_exact) / np.log(REL_MAX_DIST / max_exact)
                         * (nb - max_exact)).astype(jnp.int32)
    large = jnp.minimum(large, nb - 1)
    return ret + jnp.where(n < max_exact, n, large)


def _bias_tables(rel_bias):
    key = jnp.arange(TILE, dtype=jnp.int32)[:, None]
    qry = jnp.arange(TILE, dtype=jnp.int32)[None, :]
    rel_diag = key - qry
    far = rel_bias[_t5_bucket(jnp.int32(-(TILE + 1)))]

    def table(rel):
        bucket = _t5_bucket(rel)[None]
        out = jnp.zeros((DSA_HEADS, TILE, TILE), F32)
        for b in range(REL_BUCKETS):
            out = jnp.where(bucket == b, rel_bias[b][:, None, None], out)
        return (out - far[:, None, None]) * LOG2E

    diag = jnp.where(((key // CHUNK) <= (qry // CHUNK))[None], table(rel_diag), NEG_INF)
    return jnp.stack([table(rel_diag - TILE), diag], axis=1).astype(F32)


def _rope_tables(seq):
    half = MLA_ROPE // 2
    freqs = ROPE_BASE ** (-jnp.arange(half, dtype=jnp.float32) / half)
    ang = jnp.arange(seq, dtype=jnp.int32).astype(jnp.float32)[:, None] * freqs[None, :]
    cos, sin = jnp.cos(ang), jnp.sin(ang)
    zeros = jnp.zeros((seq, LANES - MLA_NOPE - MLA_ROPE), F32)
    lead = jnp.zeros((seq, MLA_NOPE), F32)
    c_k = jnp.concatenate([lead, cos, cos, zeros], axis=1)
    s_k = jnp.concatenate([lead, -sin, sin, zeros], axis=1)
    scale = (MLA_NOPE + MLA_ROPE) ** -0.5 * LOG2E
    c_q = jnp.concatenate([lead + 1.0, cos, cos, zeros], axis=1) * scale
    return c_q.T, (s_k * scale).T, c_k, s_k


def _prep_weights(w_in, w_uq, w_ukv):
    cuts = np.cumsum(IN_SPLITS)[:-1].tolist()
    (w_qlat, w_ckv, w_kr, w_za, w_qb, w_kb, w_vb, w_zb,
     w_qi, w_ki, w_wi, w_ga, w_gb) = jnp.split(w_in, cuts, axis=1)
    half = MLA_ROPE // 2
    swap = np.concatenate([np.arange(half, MLA_ROPE), np.arange(half)])

    def z(n):
        return jnp.zeros((D_MODEL, n), F32)

    pad = LANES - MLA_NOPE - MLA_ROPE
    kr = jnp.concatenate([z(MLA_NOPE), w_kr, z(pad)], axis=1)
    krs = jnp.concatenate([z(MLA_NOPE), w_kr[:, swap], z(pad)], axis=1)
    ki4 = jnp.tile(w_ki, (1, LANES // IDX_DIM))
    w1 = jnp.concatenate([w_qlat, w_ckv, kr, krs, w_za, w_kb, w_zb, ki4, w_ga, w_gb], axis=1)
    w2t = jnp.concatenate([w_qb, w_vb, w_qi, w_wi, z(BF16_ROWS - IDX_HEADS)], axis=1).T

    wuq = w_uq.reshape(MLA_Q_RANK, MLA_HEADS, MLA_NOPE + MLA_ROPE)
    zq = jnp.zeros((MLA_Q_RANK, MLA_HEADS, pad), F32)
    wuqm = jnp.concatenate([wuq, zq], axis=2).reshape(MLA_Q_RANK, MLA_HEADS * LANES)
    wuqs = jnp.concatenate([jnp.zeros((MLA_Q_RANK, MLA_HEADS, MLA_NOPE), F32),
                            wuq[:, :, MLA_NOPE:][:, :, swap], zq], axis=2
                           ).reshape(MLA_Q_RANK, MLA_HEADS * LANES)
    wukv = w_ukv.reshape(MLA_KV_RANK, MLA_HEADS, MLA_NOPE + MLA_V)
    wk = jnp.concatenate([wukv[:, :, :MLA_NOPE],
                          jnp.zeros((MLA_KV_RANK, MLA_HEADS, LANES - MLA_NOPE), F32)], axis=2
                         ).reshape(MLA_KV_RANK, MLA_HEADS * LANES)
    wv = wukv[:, :, MLA_NOPE:].reshape(MLA_KV_RANK, MLA_WIDTH)
    c = lambda a: a.astype(MXU_DTYPE)
    return c(w1), c(w2t), c(wuqm.T), c(wuqs.T), c(wk), c(wv.T)


def _full(shape):
    return pl.BlockSpec(shape, lambda *_: (0,) * len(shape))


def _resident(shape, index_map):
    return pl.BlockSpec(shape, index_map, pipeline_mode=pl.Buffered(1))


def kernel(x, norm_g, w_in, g_q_lat, w_uq, g_kv_lat, w_ukv, w_o_a, w_o_b, w_out, rel_bias, final_g):
    B, S, D = x.shape
    assert D == D_MODEL and S % TILE == 0 and norm_g.shape[0] == 1
    assert MLA_V == HEAD_V and DSA_HEAD_DIM == HEAD_V
    assert S // ROW_CHUNK <= 256
    nb = S // TILE
    topk = min(TOPK_MAX, S // 4)
    bf = MXU_DTYPE
    params = functools.partial(pltpu.CompilerParams, vmem_limit_bytes=VMEM_LIMIT)

    w1, w2t, wuqm_t, wuqs_t, wk, wv_t = _prep_weights(w_in[0], w_uq[0], w_ukv[0])
    cq_t, sq_t, c_k, s_k = _rope_tables(S)
    n1, n2 = w1.shape[1], w2t.shape[0]
    va_rows, vb_rows = MLA_HEADS * V_ROWS, DSA_HEADS * V_ROWS

    tok = lambda c: pl.BlockSpec((1, TILE, c), lambda b, r: (b, r, 0))
    tok_t = lambda c: pl.BlockSpec((1, c, TILE), lambda b, r: (b, 0, r))
    blk_t = lambda c: pl.BlockSpec((1, 1, c, TILE), lambda b, r: (b, r, 0, 0))
    sds = jax.ShapeDtypeStruct
    (qa_t, ka, va_t, za, qb_t, kb, vb_t, zb, qi_t, ki, wi_t, ga, gb) = pl.pallas_call(
        _proj_kernel,
        grid=(B, nb),
        in_specs=[tok(D), _full((1, D)), _full((D, n1)), _full((n2, D)),
                  _full((1, MLA_Q_RANK)), _full((MLA_HEADS * LANES, MLA_Q_RANK)),
                  _full((MLA_HEADS * LANES, MLA_Q_RANK)),
                  _full((1, MLA_KV_RANK)), _full((MLA_KV_RANK, MLA_HEADS * LANES)),
                  _full((MLA_WIDTH, MLA_KV_RANK)),
                  pl.BlockSpec((LANES, TILE), lambda b, r: (0, r)),
                  pl.BlockSpec((LANES, TILE), lambda b, r: (0, r)),
                  pl.BlockSpec((TILE, LANES), lambda b, r: (r, 0)),
                  pl.BlockSpec((TILE, LANES), lambda b, r: (r, 0))],
        out_specs=[tok_t(MLA_HEADS * LANES), tok(MLA_HEADS * LANES), blk_t(va_rows), tok(MLA_WIDTH),
                   tok_t(DSA_WIDTH), tok(DSA_WIDTH), blk_t(vb_rows), tok(DSA_WIDTH),
                   tok_t(IDX_HEADS * IDX_DIM), tok(LANES), tok_t(IDX_HEADS), tok(D), tok(D)],
        out_shape=[sds((B, MLA_HEADS * LANES, S), bf), sds((B, S, MLA_HEADS * LANES), bf),
                   sds((B, nb, va_rows, TILE), bf), sds((B, S, MLA_WIDTH), bf),
                   sds((B, DSA_WIDTH, S), bf), sds((B, S, DSA_WIDTH), bf),
                   sds((B, nb, vb_rows, TILE), bf), sds((B, S, DSA_WIDTH), bf),
                   sds((B, IDX_HEADS * IDX_DIM, S), bf), sds((B, S, LANES), bf),
                   sds((B, IDX_HEADS, S), F32), sds((B, S, D), bf), sds((B, S, D), bf)],
        compiler_params=params(dimension_semantics=("parallel", "parallel")),
        name="proj",
    )(x, norm_g, w1, w2t, g_q_lat, wuqm_t, wuqs_t, g_kv_lat, wk, wv_t, cq_t, sq_t, c_k, s_k)

    def flash_state(heads):
        return ([pltpu.VMEM((heads, 1, TILE), F32),
                 pltpu.VMEM((heads, 2, 1, TILE), F32),
                 pltpu.VMEM((heads, V_ROWS, TILE), F32),
                 pltpu.VMEM((heads * HEAD_V, TILE), F32)]
                + [pltpu.VMEM((2, TILE, TILE), F32)] * heads)

    ya = pl.pallas_call(
        _mla_kernel,
        grid=(B, nb),
        in_specs=[pl.BlockSpec((1, MLA_HEADS * LANES, TILE), lambda b, i: (b, 0, i)),
                  _resident((1, S, MLA_HEADS * LANES), lambda b, i: (b, 0, 0)),
                  _resident((1, nb, va_rows, TILE), lambda b, i: (b, 0, 0, 0)),
                  pl.BlockSpec((1, TILE, MLA_WIDTH), lambda b, i: (b, i, 0))],
        out_specs=pl.BlockSpec((1, TILE, MLA_WIDTH), lambda b, i: (b, i, 0)),
        out_shape=sds((B, S, MLA_WIDTH), bf),
        scratch_shapes=flash_state(MLA_HEADS),
        compiler_params=params(dimension_semantics=("parallel", "arbitrary")),
        name="mla",
    )(qa_t, ka, va_t, za)

    yb = pl.pallas_call(
        functools.partial(_dsa_kernel, topk),
        grid=(B, nb),
        in_specs=[pl.BlockSpec((1, DSA_WIDTH, TILE), lambda b, i: (b, 0, i)),
                  _resident((1, S, DSA_WIDTH), lambda b, i: (b, 0, 0)),
                  _resident((1, nb, vb_rows, TILE), lambda b, i: (b, 0, 0, 0)),
                  pl.BlockSpec((1, IDX_HEADS * IDX_DIM, TILE), lambda b, i: (b, 0, i)),
                  _resident((1, S, LANES), lambda b, i: (b, 0, 0)),
                  pl.BlockSpec((1, IDX_HEADS, TILE), lambda b, i: (b, 0, i)),
                  pl.BlockSpec((1, TILE, DSA_WIDTH), lambda b, i: (b, i, 0)),
                  _resident((DSA_HEADS, 2, TILE, TILE), lambda b, i: (0, 0, 0, 0))],
        out_specs=pl.BlockSpec((1, TILE, DSA_WIDTH), lambda b, i: (b, i, 0)),
        out_shape=sds((B, S, DSA_WIDTH), bf),
        scratch_shapes=[pltpu.VMEM((S, TILE), F32),
                        pltpu.VMEM((S, TILE), jnp.bfloat16),
                        pltpu.VMEM((DSA_HEADS, LANES, TILE), bf),
                        pltpu.VMEM((IDX_HEADS, LANES, TILE), bf),
                        pltpu.VMEM((2, TILE, TILE), F32),
                        pltpu.VMEM((4, 8, TILE), F32),
                        *flash_state(DSA_HEADS)],
        compiler_params=params(dimension_semantics=("parallel", "arbitrary")),
        name="dsa",
    )(qb_t, kb, vb_t, qi_t, ki, wi_t, zb, _bias_tables(rel_bias))

    return pl.pallas_call(
        _merge_kernel,
        grid=(B, nb),
        in_specs=[tok(D), tok(MLA_WIDTH), tok(DSA_WIDTH), tok(D), tok(D),
                  _full((MLA_WIDTH, D)), _full((DSA_WIDTH, D)), _full((D, D)), _full((1, D))],
        out_specs=tok(D),
        out_shape=sds((B, S, D), x.dtype),
        compiler_params=params(dimension_semantics=("parallel", "parallel")),
        name="merge",
    )(x, ya, yb, ga, gb, w_o_a[0].astype(bf), w_o_b[0].astype(bf), w_out[0].astype(bf),
      final_g.reshape(1, D))
```

```python
import functools
import math

import numpy as np
import jax
import jax.numpy as jnp
from jax import lax
from jax.experimental import pallas as pl
from jax.experimental.pallas import tpu as pltpu

D_MODEL = 1024
CHUNK = 64
EPS = 1e-6
MLA_HEADS = 8
MLA_NOPE = 64
MLA_ROPE = 32
MLA_V = 64
MLA_Q_RANK = 384
MLA_KV_RANK = 256
ROPE_BASE = 10000.0
MLA_WIDTH = MLA_HEADS * MLA_V
DSA_HEADS = 8
DSA_HEAD_DIM = 64
DSA_WIDTH = DSA_HEADS * DSA_HEAD_DIM
IDX_HEADS = 8
IDX_DIM = 32
TOPK_MAX = 256
REL_BUCKETS = 32
REL_MAX_DIST = 128
IN_SPLITS = (MLA_Q_RANK, MLA_KV_RANK, MLA_ROPE, MLA_WIDTH,
             DSA_WIDTH, DSA_WIDTH, DSA_WIDTH, DSA_WIDTH,
             IDX_HEADS * IDX_DIM, IDX_DIM, IDX_HEADS,
             D_MODEL, D_MODEL)

LANES = 128
BF16_ROWS = 16
TILE = 256
ROW_CHUNK = 64
HEAD_V = 64
V_ROWS = HEAD_V + BF16_ROWS
VMEM_LIMIT = 56 * 1024 * 1024
MXU_DTYPE = jnp.bfloat16
F32 = jnp.float32
NEG_INF = float("-inf")
M_INIT = -1e30
LOG2E = math.log2(math.e)

KEY_LO = int(np.int32(np.uint32(0x80800000)))
KEY_GRID = 1 << 16


def _nt(a, b):
    return lax.dot_general(a, b, (((1,), (1,)), ((), ())), preferred_element_type=F32)


def _nn(a, b):
    return jnp.dot(a, b, preferred_element_type=F32)


def _sigmoid(v):
    return 1.0 / (1.0 + jnp.exp(-v))


def _rms(v, g):
    return v * lax.rsqrt(jnp.mean(v * v, axis=-1, keepdims=True) + EPS) * g


def _store_values(vt_ref, v_t):
    row = lax.broadcasted_iota(jnp.int32, (BF16_ROWS, TILE), 0)
    ones_row = jnp.where(row == 0, 1.0, 0.0).astype(MXU_DTYPE)
    for h in range(v_t.shape[0] // HEAD_V):
        vt_ref[0, 0, V_ROWS * h:V_ROWS * h + HEAD_V, :] = v_t[HEAD_V * h:HEAD_V * (h + 1)].astype(MXU_DTYPE)
        vt_ref[0, 0, V_ROWS * h + HEAD_V:V_ROWS * (h + 1), :] = ones_row


def _proj_kernel(x_ref, ng_ref, w1_ref, w2t_ref, gq_ref, wuqm_ref, wuqs_ref, gkv_ref, wk_ref,
                 wvt_ref, cqt_ref, sqt_ref, ck_ref, sk_ref,
                 qat_ref, ka_ref, vat_ref, za_ref, qbt_ref, kb_ref, vbt_ref, zb_ref,
                 qit_ref, ki_ref, wit_ref, ga_ref, gb_ref):
    hb = _rms(x_ref[0], ng_ref[...]).astype(MXU_DTYPE)

    def cols(lo, hi):
        return _nn(hb, w1_ref[:, lo:hi])

    def rows(lo, hi):
        return _nt(w2t_ref[lo:hi, :], hb)

    qn = _rms(cols(0, 384), gq_ref[...]).astype(MXU_DTYPE)
    qm = _nt(wuqm_ref[...], qn)
    qs = _nt(wuqs_ref[...], qn)
    cqt = cqt_ref[...]
    sqt = sqt_ref[...]
    for h in range(MLA_HEADS):
        sl = slice(LANES * h, LANES * (h + 1))
        qat_ref[0, sl, :] = (qm[sl] * cqt + qs[sl] * sqt).astype(MXU_DTYPE)

    kvn = _rms(cols(384, 640), gkv_ref[...]).astype(MXU_DTYPE)
    kn = _nn(kvn, wk_ref[...])
    kpe = cols(640, 768) * ck_ref[...] + cols(768, 896) * sk_ref[...]
    for h in range(MLA_HEADS):
        sl = slice(LANES * h, LANES * (h + 1))
        ka_ref[0, :, sl] = (kn[:, sl] + kpe).astype(MXU_DTYPE)
    _store_values(vat_ref, _nt(wvt_ref[...], kvn))

    za = cols(896, 1408)
    za_ref[0] = (za * _sigmoid(za)).astype(MXU_DTYPE)
    kb_ref[0] = cols(1408, 1920).astype(MXU_DTYPE)
    zb = cols(1920, 2432)
    zb_ref[0] = (zb * _sigmoid(zb)).astype(MXU_DTYPE)
    ki_ref[0] = cols(2432, 2560).astype(MXU_DTYPE)
    ga_ref[0] = _sigmoid(cols(2560, 3584)).astype(MXU_DTYPE)
    gb_ref[0] = _sigmoid(cols(3584, 4608)).astype(MXU_DTYPE)

    qbt_ref[0] = (rows(0, 512) * (DSA_HEAD_DIM ** -0.5 * LOG2E)).astype(MXU_DTYPE)
    _store_values(vbt_ref, rows(512, 1024))
    qit_ref[0] = rows(1024, 1280).astype(MXU_DTYPE)
    wit_ref[0] = rows(1280, 1296)[:IDX_HEADS] * ((IDX_DIM * IDX_HEADS) ** -0.5)


def _diag_admissible():
    key = lax.broadcasted_iota(jnp.int32, (TILE, TILE), 0)
    qry = lax.broadcasted_iota(jnp.int32, (TILE, TILE), 1)
    return (key // CHUNK) <= (qry // CHUNK)


def _flash_init(m_ref, acc_ref):
    m_ref[...] = jnp.full(m_ref.shape, M_INIT, F32)
    acc_ref[...] = jnp.zeros(acc_ref.shape, F32)


class _Flash:
    def __init__(self, n, score, value, m_ref, mb_ref, acc_ref, s_refs, prepare=None):
        self.n, self.score, self.value, self.prepare = n, score, value, prepare
        self.m_ref, self.mb_ref, self.acc_ref, self.s_refs = m_ref, mb_ref, acc_ref, s_refs

    def put_scores(self, h, slot, s):
        self.s_refs[h][slot] = s
        mx = s[0:ROW_CHUNK]
        for c in range(1, TILE // ROW_CHUNK):
            mx = jnp.maximum(mx, s[ROW_CHUNK * c:ROW_CHUNK * (c + 1)])
        self.mb_ref[h, slot] = jnp.max(mx, axis=0, keepdims=True)

    def scores(self, j, slot, score=None):
        if self.prepare is not None:
            self.prepare(j, slot)
        for h in range(self.n):
            self.put_scores(h, slot, (score or self.score)(h, j, slot))

    def softmax_values(self, j, slot, nxt=None, nxt_score=None):
        rows = ROW_CHUNK
        n_chunks = TILE // rows
        if nxt is not None and self.prepare is not None:
            self.prepare(*nxt)
        for h in range(self.n):
            if nxt is not None:
                self.put_scores(h, nxt[1], (nxt_score or self.score)(h, *nxt))
            s_ref = self.s_refs[h]
            m_old = self.m_ref[h]
            m_new = jnp.maximum(m_old, self.mb_ref[h, slot])
            p = jnp.concatenate(
                [jnp.exp2(s_ref[slot, rows * c:rows * (c + 1)] - m_new).astype(MXU_DTYPE)
                 for c in range(n_chunks)], axis=0)
            self.m_ref[h] = m_new
            alpha = jnp.exp2(m_old - m_new)
            self.acc_ref[h] = alpha * self.acc_ref[h] + _nn(self.value(h, j), p)

    def pipeline(self, n_blocks, first_slot):
        _pipeline(n_blocks, lambda: None,
                  lambda j, slot: self.softmax_values(j, slot, nxt=(j + 1, 1 - slot)),
                  self.softmax_values, first_slot)


def _pipeline(n_blocks, first, step, last, first_slot=0):
    s0, s1 = first_slot, 1 - first_slot

    @pl.when(n_blocks > 0)
    def _():
        first()
        n_pairs = (n_blocks - 1) // 2

        def body(t, c):
            step(2 * t, s0)
            step(2 * t + 1, s1)
            return c

        lax.fori_loop(0, n_pairs, body, 0)
        j = 2 * n_pairs

        @pl.when(j + 1 < n_blocks)
        def _():
            step(j, s0)
            last(j + 1, s1)

        @pl.when(j + 1 == n_blocks)
        def _():
            last(j, s0)


def _flash_finish(n_heads, z_ref, o_ref, acc_ref, y_ref):
    for h in range(n_heads):
        y_ref[HEAD_V * h:HEAD_V * (h + 1), :] = acc_ref[h, 0:HEAD_V] / acc_ref[h, HEAD_V:HEAD_V + 1]
    o_ref[0] = (y_ref[...].T * z_ref[0].astype(F32)).astype(MXU_DTYPE)


def _mla_kernel(qt_ref, k_ref, vt_ref, z_ref, o_ref, m_ref, mb_ref, acc_ref, y_ref, *s_refs):
    i = pl.program_id(1)
    _flash_init(m_ref, acc_ref)

    def score(h, j, slot):
        sl = slice(LANES * h, LANES * (h + 1))
        off = pl.multiple_of(j * TILE, TILE)
        return _nn(k_ref[0, pl.ds(off, TILE), sl], qt_ref[0, sl, :])

    def diag_score(h, j, slot):
        return jnp.where(_diag_admissible(), score(h, j, slot), NEG_INF)

    def value(h, j):
        return vt_ref[0, j, V_ROWS * h:V_ROWS * (h + 1), :]

    flash = _Flash(MLA_HEADS, score, value, m_ref, mb_ref, acc_ref, s_refs)
    flash.scores(i, 0, diag_score)

    @pl.when(i == 0)
    def _():
        flash.softmax_values(i, 0)

    @pl.when(i >= 1)
    def _():
        flash.softmax_values(i, 0, nxt=(0, 1))
        flash.pipeline(i, first_slot=1)

    _flash_finish(MLA_HEADS, z_ref, o_ref, acc_ref, y_ref)


def _float_key(v):
    bits = lax.bitcast_convert_type(v, jnp.int32)
    return bits ^ ((bits >> 31) & 0x7FFFFFFF)


def _key_float(k):
    return lax.bitcast_convert_type(k ^ ((k >> 31) & 0x7FFFFFFF), F32)


def _top_half(v):
    bits = lax.bitcast_convert_type(v, jnp.int32) & -KEY_GRID
    return lax.bitcast_convert_type(bits, F32)


def _any(mask):
    return jnp.max(jnp.where(mask, 1.0, 0.0)) > 0.5


def _dsa_kernel(topk, qbt_ref, kb_ref, vbt_ref, qit_ref, ki_ref, wit_ref, zb_ref, bias_ref, o_ref,
                sc_ref, sc16_ref, qm_ref, qim_ref, sel_ref, lohi_ref, m_ref, mb_ref, acc_ref, y_ref,
                *s_refs):
    i = pl.program_id(1)
    adm = _diag_admissible()
    n_chunks = TILE // ROW_CHUNK

    row = lax.broadcasted_iota(jnp.int32, (LANES, TILE), 0)
    for h in range(DSA_HEADS):
        g = h // 2
        qm_ref[h] = jnp.where((row // DSA_HEAD_DIM) == (h % 2),
                              qbt_ref[0, LANES * g:LANES * (g + 1), :], 0).astype(MXU_DTYPE)
        gi = h // 4
        qim_ref[h] = jnp.where((row // IDX_DIM) == (h % 4),
                               qit_ref[0, LANES * gi:LANES * (gi + 1), :], 0).astype(MXU_DTYPE)

    def fold8(op, acc, v):
        for r in range(v.shape[0] // 8):
            acc = op(acc, v[8 * r:8 * (r + 1)])
        return acc

    heads_per_chunk = IDX_HEADS // n_chunks

    def stage_dots(j, slot, heads):
        off = pl.multiple_of(j * TILE, TILE)
        kij = ki_ref[0, pl.ds(off, TILE), :]
        for h in heads:
            s_refs[h][slot] = _nn(kij, qim_ref[h])

    def fold_scores(j, slot, masked=False, nxt=None):
        off = pl.multiple_of(j * TILE, TILE)
        lo8, hi8, pos8, nonneg8 = (lohi_ref[k] for k in range(4))
        for c in range(n_chunks):
            if nxt is not None:
                stage_dots(*nxt, range(heads_per_chunk * c, heads_per_chunk * (c + 1)))
            rs = slice(ROW_CHUNK * c, ROW_CHUNK * (c + 1))
            tot = jnp.zeros((ROW_CHUNK, TILE), F32)
            for h in range(IDX_HEADS):
                tot = tot + jnp.maximum(s_refs[h][slot, rs], 0.0) * wit_ref[0, h:h + 1, :]
            if masked:
                lo8 = fold8(jnp.minimum, lo8, jnp.where(adm[rs], tot, -NEG_INF))
                tot = jnp.where(adm[rs], tot, NEG_INF)
            else:
                lo8 = fold8(jnp.minimum, lo8, tot)
            hi8 = fold8(jnp.maximum, hi8, tot)
            pos8 = fold8(jnp.add, pos8, jnp.where(tot > 0.0, 1.0, 0.0))
            nonneg8 = fold8(jnp.add, nonneg8, jnp.where(tot >= 0.0, 1.0, 0.0))
            rows = pl.ds(pl.multiple_of(off + ROW_CHUNK * c, ROW_CHUNK), ROW_CHUNK)
            sc_ref[rows, :] = tot
            sc16_ref[rows, :] = _top_half(tot).astype(jnp.bfloat16)
        for k, v in enumerate((lo8, hi8, pos8, nonneg8)):
            lohi_ref[k] = v

    lohi_ref[0] = jnp.full((8, TILE), -NEG_INF, F32)
    lohi_ref[1] = jnp.full((8, TILE), NEG_INF, F32)
    lohi_ref[2] = jnp.zeros((8, TILE), F32)
    lohi_ref[3] = jnp.zeros((8, TILE), F32)
    stage_dots(i, 0, range(IDX_HEADS))
    fold_scores(i, 0, masked=True)
    _pipeline(i, lambda: stage_dots(0, 0, range(IDX_HEADS)),
              lambda j, slot: fold_scores(j, slot, nxt=(j + 1, 1 - slot)),
              fold_scores)
    smin = jnp.min(lohi_ref[0], axis=0, keepdims=True)
    smax = jnp.max(lohi_ref[1], axis=0, keepdims=True)

    def count(pred):
        def body(jb, acc):
            for c in range(n_chunks):
                off = pl.multiple_of(jb * TILE + c * ROW_CHUNK, ROW_CHUNK)
                acc = acc + jnp.where(pred(sc_ref[pl.ds(off, ROW_CHUNK), :], off), 1.0, 0.0)
            return acc
        acc = lax.fori_loop(0, i + 1, body, jnp.zeros((ROW_CHUNK, TILE), F32))
        return jnp.sum(acc, axis=0, keepdims=True)

    def mid_of(lo, hi):
        return (lo >> 1) + (hi >> 1) + (lo & hi & 1)

    kf = float(topk)
    qpos = i * TILE + lax.broadcasted_iota(jnp.int32, (1, TILE), 1)
    n_adm = ((qpos // CHUNK + 1) * CHUNK).astype(F32)
    few = n_adm <= kf

    def count16(thr16):
        one, zero = jnp.ones((), jnp.bfloat16), jnp.zeros((), jnp.bfloat16)

        def body(jb, acc):
            for c in range(n_chunks):
                off = pl.multiple_of(jb * TILE + c * ROW_CHUNK, ROW_CHUNK)
                acc = acc + jnp.where(sc16_ref[pl.ds(off, ROW_CHUNK), :] >= thr16, one, zero)
            return acc
        acc = lax.fori_loop(0, i + 1, body, jnp.zeros((ROW_CHUNK, TILE), jnp.bfloat16))
        return jnp.sum(acc.astype(F32), axis=0, keepdims=True)

    def probe_of(lo, hi):
        margin = (hi >> 3) - (lo >> 3)
        probe = _float_key(0.5 * _key_float(lo) + 0.5 * _key_float(hi))
        probe = jnp.clip(probe, lo + margin, hi - margin)
        return jnp.clip(probe, lo + 1, hi - 1)

    def grid_probe(lo, hi):
        near = (probe_of(lo, hi) + (KEY_GRID // 2)) & -KEY_GRID
        above_lo = ((lo >> 16) + 1) << 16
        probe = jnp.where(jnp.logical_and(near > lo, near < hi), near, above_lo)
        return probe, jnp.logical_and(mid_of(lo, hi) != lo, probe < hi)

    def update(state, probe, c, active):
        lo, hi, cnt = state
        ge = c >= kf
        take_lo = jnp.logical_and(active, ge)
        new_hi = jnp.where(jnp.logical_and(active, jnp.logical_not(ge)), probe, hi)
        new_hi = jnp.where(jnp.logical_and(active, c == kf), probe + 1, new_hi)
        return jnp.where(take_lo, probe, lo), new_hi, jnp.where(take_lo, c, cnt)

    def coarse_pass(state):
        probe, active = grid_probe(state[0], state[1])
        thr16 = _top_half(_key_float(probe)).astype(jnp.bfloat16)
        return update(state, probe, count16(thr16), active)

    def fine_pass(state):
        lo, hi, _ = state
        probe = probe_of(lo, hi)
        thr = _key_float(probe)
        return update(state, probe, count(lambda blk, off: blk >= thr), mid_of(lo, hi) != lo)

    def search(one_pass, is_active, state, per_check):
        def cond(c):
            return jnp.logical_and(c[0] < 128, _any(is_active(c[1], c[2])))

        def body(c):
            st = c[1:]
            for _ in range(per_check):
                st = one_pass(st)
            return (c[0] + per_check,) + st

        return lax.while_loop(cond, body, (jnp.int32(0),) + state)[1:]

    c_pos = jnp.sum(lohi_ref[2], axis=0, keepdims=True)
    c_nonneg = jnp.sum(lohi_ref[3], axis=0, keepdims=True)
    at_zero = jnp.logical_and(c_pos < kf, kf <= c_nonneg)
    above = kf <= c_pos
    lo0 = jnp.where(at_zero, 0, jnp.where(above, 1, _float_key(smin)))
    hi0 = jnp.where(at_zero, 1, jnp.where(above, _float_key(smax) + 1, -1))
    cnt0 = jnp.where(at_zero, c_nonneg, jnp.where(above, c_pos, n_adm))
    lo0 = jnp.where(few, KEY_LO, lo0)
    hi0 = jnp.where(few, KEY_LO + 1, hi0)
    cnt0 = jnp.where(few, n_adm, cnt0)
    state = search(coarse_pass, lambda lo, hi: grid_probe(lo, hi)[1], (lo0, hi0, cnt0), 4)
    lo, _, cnt = search(fine_pass, lambda lo, hi: mid_of(lo, hi) != lo, state, 2)
    thr = _key_float(lo)

    excess = cnt > kf

    @pl.when(_any(excess))
    def _():
        need = kf - count(lambda blk, off: blk > thr)
        key = lax.broadcasted_iota(jnp.int32, (TILE, TILE), 0)
        upto = lax.broadcasted_iota(jnp.int32, (TILE, TILE), 1)
        prefix = jnp.where(upto <= key, 1.0, 0.0).astype(MXU_DTYPE)

        def drop(jb, seen):
            off = pl.multiple_of(jb * TILE, TILE)
            blk = sc_ref[pl.ds(off, TILE), :]
            tie = blk == thr
            rank = seen + _nn(prefix, jnp.where(tie, 1.0, 0.0).astype(MXU_DTYPE))
            sc_ref[pl.ds(off, TILE), :] = jnp.where(jnp.logical_and(tie, rank > need), NEG_INF, blk)
            return rank[TILE - 1:TILE, :]

        lax.fori_loop(0, i + 1, drop, jnp.zeros((1, TILE), F32))

    _flash_init(m_ref, acc_ref)

    def select(j, slot):
        off = pl.multiple_of(j * TILE, TILE)
        sel_ref[slot] = jnp.where(sc_ref[pl.ds(off, TILE), :] >= thr, 0.0, NEG_INF)

    def score(h, j, slot, near=None):
        g = h // 2
        off = pl.multiple_of(j * TILE, TILE)
        s = _nn(kb_ref[0, pl.ds(off, TILE), LANES * g:LANES * (g + 1)], qm_ref[h])
        if near is not None:
            s = s + bias_ref[h, near]
        return s + sel_ref[slot]

    def value(h, j):
        return vbt_ref[0, j, V_ROWS * h:V_ROWS * (h + 1), :]

    flash = _Flash(DSA_HEADS, score, value, m_ref, mb_ref, acc_ref, s_refs, prepare=select)
    flash.scores(i, 0, functools.partial(score, near=1))

    @pl.when(i == 0)
    def _():
        flash.softmax_values(i, 0)

    @pl.when(i >= 1)
    def _():
        flash.softmax_values(i, 0, nxt=(i - 1, 1), nxt_score=functools.partial(score, near=0))

        @pl.when(i == 1)
        def _():
            flash.softmax_values(i - 1, 1)

        @pl.when(i >= 2)
        def _():
            flash.softmax_values(i - 1, 1, nxt=(0, 0))
            flash.pipeline(i - 1, first_slot=0)

    _flash_finish(DSA_HEADS, zb_ref, o_ref, acc_ref, y_ref)


def _merge_kernel(x_ref, ya_ref, yb_ref, ga_ref, gb_ref, woa_ref, wob_ref, wout_ref, fg_ref, o_ref):
    merged = (ga_ref[0].astype(F32) * _nn(ya_ref[0], woa_ref[...])
              + gb_ref[0].astype(F32) * _nn(yb_ref[0], wob_ref[...]))
    y = x_ref[0] + _nn(merged.astype(MXU_DTYPE), wout_ref[...])
    o_ref[0] = _rms(y, fg_ref[...])


def _t5_bucket(rel):
    nb = REL_BUCKETS // 2
    max_exact = nb // 2
    ret = (rel > 0).astype(jnp.int32) * nb
    n = jnp.abs(rel)
    nf = jnp.maximum(n, 1).astype(jnp.float32)
    large = max_exact + (jnp.log(nf / max_exact) / np.log(REL_MAX_DIST / max_exact)
                         * (nb - max_exact)).astype(jnp.int32)
    large = jnp.minimum(large, nb - 1)
    return ret + jnp.where(n < max_exact, n, large)


def _bias_tables(rel_bias):
    key = jnp.arange(TILE, dtype=jnp.int32)[:, None]
    qry = jnp.arange(TILE, dtype=jnp.int32)[None, :]
    rel_diag = key - qry
    far = rel_bias[_t5_bucket(jnp.int32(-(TILE + 1)))]

    def table(rel):
        bucket = _t5_bucket(rel)[None]
        out = jnp.zeros((DSA_HEADS, TILE, TILE), F32)
        for b in range(REL_BUCKETS):
            out = jnp.where(bucket == b, rel_bias[b][:, None, None], out)
        return (out - far[:, None, None]) * LOG2E

    diag = jnp.where(((key // CHUNK) <= (qry // CHUNK))[None], table(rel_diag), NEG_INF)
    return jnp.stack([table(rel_diag - TILE), diag], axis=1).astype(F32)


def _rope_tables(seq):
    half = MLA_ROPE // 2
    freqs = ROPE_BASE ** (-jnp.arange(half, dtype=jnp.float32) / half)
    ang = jnp.arange(seq, dtype=jnp.int32).astype(jnp.float32)[:, None] * freqs[None, :]
    cos, sin = jnp.cos(ang), jnp.sin(ang)
    zeros = jnp.zeros((seq, LANES - MLA_NOPE - MLA_ROPE), F32)
    lead = jnp.zeros((seq, MLA_NOPE), F32)
    c_k = jnp.concatenate([lead, cos, cos, zeros], axis=1)
    s_k = jnp.concatenate([lead, -sin, sin, zeros], axis=1)
    scale = (MLA_NOPE + MLA_ROPE) ** -0.5 * LOG2E
    c_q = jnp.concatenate([lead + 1.0, cos, cos, zeros], axis=1) * scale
    return c_q.T, (s_k * scale).T, c_k, s_k


def _prep_weights(w_in, w_uq, w_ukv):
    cuts = np.cumsum(IN_SPLITS)[:-1].tolist()
    (w_qlat, w_ckv, w_kr, w_za, w_qb, w_kb, w_vb, w_zb,
     w_qi, w_ki, w_wi, w_ga, w_gb) = jnp.split(w_in, cuts, axis=1)
    half = MLA_ROPE // 2
    swap = np.concatenate([np.arange(half, MLA_ROPE), np.arange(half)])

    def z(n):
        return jnp.zeros((D_MODEL, n), F32)

    pad = LANES - MLA_NOPE - MLA_ROPE
    kr = jnp.concatenate([z(MLA_NOPE), w_kr, z(pad)], axis=1)
    krs = jnp.concatenate([z(MLA_NOPE), w_kr[:, swap], z(pad)], axis=1)
    ki4 = jnp.tile(w_ki, (1, LANES // IDX_DIM))
    w1 = jnp.concatenate([w_qlat, w_ckv, kr, krs, w_za, w_kb, w_zb, ki4, w_ga, w_gb], axis=1)
    w2t = jnp.concatenate([w_qb, w_vb, w_qi, w_wi, z(BF16_ROWS - IDX_HEADS)], axis=1).T

    wuq = w_uq.reshape(MLA_Q_RANK, MLA_HEADS, MLA_NOPE + MLA_ROPE)
    zq = jnp.zeros((MLA_Q_RANK, MLA_HEADS, pad), F32)
    wuqm = jnp.concatenate([wuq, zq], axis=2).reshape(MLA_Q_RANK, MLA_HEADS * LANES)
    wuqs = jnp.concatenate([jnp.zeros((MLA_Q_RANK, MLA_HEADS, MLA_NOPE), F32),
                            wuq[:, :, MLA_NOPE:][:, :, swap], zq], axis=2
                           ).reshape(MLA_Q_RANK, MLA_HEADS * LANES)
    wukv = w_ukv.reshape(MLA_KV_RANK, MLA_HEADS, MLA_NOPE + MLA_V)
    wk = jnp.concatenate([wukv[:, :, :MLA_NOPE],
                          jnp.zeros((MLA_KV_RANK, MLA_HEADS, LANES - MLA_NOPE), F32)], axis=2
                         ).reshape(MLA_KV_RANK, MLA_HEADS * LANES)
    wv = wukv[:, :, MLA_NOPE:].reshape(MLA_KV_RANK, MLA_WIDTH)
    c = lambda a: a.astype(MXU_DTYPE)
    return c(w1), c(w2t), c(wuqm.T), c(wuqs.T), c(wk), c(wv.T)


def _full(shape):
    return pl.BlockSpec(shape, lambda *_: (0,) * len(shape))


def _resident(shape, index_map):
    return pl.BlockSpec(shape, index_map, pipeline_mode=pl.Buffered(1))


def kernel(x, norm_g, w_in, g_q_lat, w_uq, g_kv_lat, w_ukv, w_o_a, w_o_b, w_out, rel_bias, final_g):
    B, S, D = x.shape
    assert D == D_MODEL and S % TILE == 0 and norm_g.shape[0] == 1
    assert MLA_V == HEAD_V and DSA_HEAD_DIM == HEAD_V
    assert S // ROW_CHUNK <= 256
    nb = S // TILE
    topk = min(TOPK_MAX, S // 4)
    bf = MXU_DTYPE
    params = functools.partial(pltpu.CompilerParams, vmem_limit_bytes=VMEM_LIMIT)

    w1, w2t, wuqm_t, wuqs_t, wk, wv_t = _prep_weights(w_in[0], w_uq[0], w_ukv[0])
    cq_t, sq_t, c_k, s_k = _rope_tables(S)
    n1, n2 = w1.shape[1], w2t.shape[0]
    va_rows, vb_rows = MLA_HEADS * V_ROWS, DSA_HEADS * V_ROWS

    tok = lambda c: pl.BlockSpec((1, TILE, c), lambda b, r: (b, r, 0))
    tok_t = lambda c: pl.BlockSpec((1, c, TILE), lambda b, r: (b, 0, r))
    blk_t = lambda c: pl.BlockSpec((1, 1, c, TILE), lambda b, r: (b, r, 0, 0))
    sds = jax.ShapeDtypeStruct
    (qa_t, ka, va_t, za, qb_t, kb, vb_t, zb, qi_t, ki, wi_t, ga, gb) = pl.pallas_call(
        _proj_kernel,
        grid=(B, nb),
        in_specs=[tok(D), _full((1, D)), _full((D, n1)), _full((n2, D)),
                  _full((1, MLA_Q_RANK)), _full((MLA_HEADS * LANES, MLA_Q_RANK)),
                  _full((MLA_HEADS * LANES, MLA_Q_RANK)),
                  _full((1, MLA_KV_RANK)), _full((MLA_KV_RANK, MLA_HEADS * LANES)),
                  _full((MLA_WIDTH, MLA_KV_RANK)),
                  pl.BlockSpec((LANES, TILE), lambda b, r: (0, r)),
                  pl.BlockSpec((LANES, TILE), lambda b, r: (0, r)),
                  pl.BlockSpec((TILE, LANES), lambda b, r: (r, 0)),
                  pl.BlockSpec((TILE, LANES), lambda b, r: (r, 0))],
        out_specs=[tok_t(MLA_HEADS * LANES), tok(MLA_HEADS * LANES), blk_t(va_rows), tok(MLA_WIDTH),
                   tok_t(DSA_WIDTH), tok(DSA_WIDTH), blk_t(vb_rows), tok(DSA_WIDTH),
                   tok_t(IDX_HEADS * IDX_DIM), tok(LANES), tok_t(IDX_HEADS), tok(D), tok(D)],
        out_shape=[sds((B, MLA_HEADS * LANES, S), bf), sds((B, S, MLA_HEADS * LANES), bf),
                   sds((B, nb, va_rows, TILE), bf), sds((B, S, MLA_WIDTH), bf),
                   sds((B, DSA_WIDTH, S), bf), sds((B, S, DSA_WIDTH), bf),
                   sds((B, nb, vb_rows, TILE), bf), sds((B, S, DSA_WIDTH), bf),
                   sds((B, IDX_HEADS * IDX_DIM, S), bf), sds((B, S, LANES), bf),
                   sds((B, IDX_HEADS, S), F32), sds((B, S, D), bf), sds((B, S, D), bf)],
        compiler_params=params(dimension_semantics=("parallel", "parallel")),
        name="proj",
    )(x, norm_g, w1, w2t, g_q_lat, wuqm_t, wuqs_t, g_kv_lat, wk, wv_t, cq_t, sq_t, c_k, s_k)

    def flash_state(heads):
        return ([pltpu.VMEM((heads, 1, TILE), F32),
                 pltpu.VMEM((heads, 2, 1, TILE), F32),
                 pltpu.VMEM((heads, V_ROWS, TILE), F32),
                 pltpu.VMEM((heads * HEAD_V, TILE), F32)]
                + [pltpu.VMEM((2, TILE, TILE), F32)] * heads)

    ya = pl.pallas_call(
        _mla_kernel,
        grid=(B, nb),
        in_specs=[pl.BlockSpec((1, MLA_HEADS * LANES, TILE), lambda b, i: (b, 0, i)),
                  _resident((1, S, MLA_HEADS * LANES), lambda b, i: (b, 0, 0)),
                  _resident((1, nb, va_rows, TILE), lambda b, i: (b, 0, 0, 0)),
                  pl.BlockSpec((1, TILE, MLA_WIDTH), lambda b, i: (b, i, 0))],
        out_specs=pl.BlockSpec((1, TILE, MLA_WIDTH), lambda b, i: (b, i, 0)),
        out_shape=sds((B, S, MLA_WIDTH), bf),
        scratch_shapes=flash_state(MLA_HEADS),
        compiler_params=params(dimension_semantics=("parallel", "arbitrary")),
        name="mla",
    )(qa_t, ka, va_t, za)

    yb = pl.pallas_call(
        functools.partial(_dsa_kernel, topk),
        grid=(B, nb),
        in_specs=[pl.BlockSpec((1, DSA_WIDTH, TILE), lambda b, i: (b, 0, i)),
                  _resident((1, S, DSA_WIDTH), lambda b, i: (b, 0, 0)),
                  _resident((1, nb, vb_rows, TILE), lambda b, i: (b, 0, 0, 0)),
                  pl.BlockSpec((1, IDX_HEADS * IDX_DIM, TILE), lambda b, i: (b, 0, i)),
                  _resident((1, S, LANES), lambda b, i: (b, 0, 0)),
                  pl.BlockSpec((1, IDX_HEADS, TILE), lambda b, i: (b, 0, i)),
                  pl.BlockSpec((1, TILE, DSA_WIDTH), lambda b, i: (b, i, 0)),
                  _resident((DSA_HEADS, 2, TILE, TILE), lambda b, i: (0, 0, 0, 0))],
        out_specs=pl.BlockSpec((1, TILE, DSA_WIDTH), lambda b, i: (b, i, 0)),
        out_shape=sds((B, S, DSA_WIDTH), bf),
        scratch_shapes=[pltpu.VMEM((S, TILE), F32),
                        pltpu.VMEM((S, TILE), jnp.bfloat16),
                        pltpu.VMEM((DSA_HEADS, LANES, TILE), bf),
                        pltpu.VMEM((IDX_HEADS, LANES, TILE), bf),
                        pltpu.VMEM((2, TILE, TILE), F32),
                        pltpu.VMEM((4, 8, TILE), F32),
                        *flash_state(DSA_HEADS)],
        compiler_params=params(dimension_semantics=("parallel", "arbitrary")),
        name="dsa",
    )(qb_t, kb, vb_t, qi_t, ki, wi_t, zb, _bias_tables(rel_bias))

    return pl.pallas_call(
        _merge_kernel,
        grid=(B, nb),
        in_specs=[tok(D), tok(MLA_WIDTH), tok(DSA_WIDTH), tok(D), tok(D),
                  _full((MLA_WIDTH, D)), _full((DSA_WIDTH, D)), _full((D, D)), _full((1, D))],
        out_specs=tok(D),
        out_shape=sds((B, S, D), x.dtype),
        compiler_params=params(dimension_semantics=("parallel", "parallel")),
        name="merge",
    )(x, ya, yb, ga, gb, w_o_a[0].astype(bf), w_o_b[0].astype(bf), w_out[0].astype(bf),
      final_g.reshape(1, D))
```

```python
import functools
import math

import numpy as np
import jax
import jax.numpy as jnp
from jax import lax
from jax.experimental import pallas as pl
from jax.experimental.pallas import tpu as pltpu

D_MODEL = 1024
CHUNK = 64
EPS = 1e-6
MLA_HEADS = 8
MLA_NOPE = 64
MLA_ROPE = 32
MLA_V = 64
MLA_Q_RANK = 384
MLA_KV_RANK = 256
ROPE_BASE = 10000.0
MLA_WIDTH = MLA_HEADS * MLA_V
DSA_HEADS = 8
DSA_HEAD_DIM = 64
DSA_WIDTH = DSA_HEADS * DSA_HEAD_DIM
IDX_HEADS = 8
IDX_DIM = 32
TOPK_MAX = 256
REL_BUCKETS = 32
REL_MAX_DIST = 128
IN_SPLITS = (MLA_Q_RANK, MLA_KV_RANK, MLA_ROPE, MLA_WIDTH,
             DSA_WIDTH, DSA_WIDTH, DSA_WIDTH, DSA_WIDTH,
             IDX_HEADS * IDX_DIM, IDX_DIM, IDX_HEADS,
             D_MODEL, D_MODEL)

LANES = 128
BF16_ROWS = 16
TILE = 256
ROW_CHUNK = 64
HEAD_V = 64
V_ROWS = HEAD_V + BF16_ROWS
VMEM_LIMIT = 56 * 1024 * 1024
MXU_DTYPE = jnp.bfloat16
F32 = jnp.float32
NEG_INF = float("-inf")
M_INIT = -1e30
LOG2E = math.log2(math.e)

KEY_LO = int(np.int32(np.uint32(0x80800000)))
KEY_GRID = 1 << 16


def _nt(a, b):
    return lax.dot_general(a, b, (((1,), (1,)), ((), ())), preferred_element_type=F32)


def _nn(a, b):
    return jnp.dot(a, b, preferred_element_type=F32)


def _sigmoid(v):
    return 1.0 / (1.0 + jnp.exp(-v))


def _rms(v, g):
    return v * lax.rsqrt(jnp.mean(v * v, axis=-1, keepdims=True) + EPS) * g


def _store_values(vt_ref, v_t):
    row = lax.broadcasted_iota(jnp.int32, (BF16_ROWS, TILE), 0)
    ones_row = jnp.where(row == 0, 1.0, 0.0).astype(MXU_DTYPE)
    for h in range(v_t.shape[0] // HEAD_V):
        vt_ref[0, 0, V_ROWS * h:V_ROWS * h + HEAD_V, :] = v_t[HEAD_V * h:HEAD_V * (h + 1)].astype(MXU_DTYPE)
        vt_ref[0, 0, V_ROWS * h + HEAD_V:V_ROWS * (h + 1), :] = ones_row


def _proj_kernel(x_ref, ng_ref, w1_ref, w2t_ref, gq_ref, wuqm_ref, wuqs_ref, gkv_ref, wk_ref,
                 wvt_ref, cqt_ref, sqt_ref, ck_ref, sk_ref,
                 qat_ref, ka_ref, vat_ref, za_ref, qbt_ref, kb_ref, vbt_ref, zb_ref,
                 qit_ref, ki_ref, wit_ref, ga_ref, gb_ref):
    hb = _rms(x_ref[0], ng_ref[...]).astype(MXU_DTYPE)

    def cols(lo, hi):
        return _nn(hb, w1_ref[:, lo:hi])

    def rows(lo, hi):
        return _nt(w2t_ref[lo:hi, :], hb)

    qn = _rms(cols(0, 384), gq_ref[...]).astype(MXU_DTYPE)
    qm = _nt(wuqm_ref[...], qn)
    qs = _nt(wuqs_ref[...], qn)
    cqt = cqt_ref[...]
    sqt = sqt_ref[...]
    for h in range(MLA_HEADS):
        sl = slice(LANES * h, LANES * (h + 1))
        qat_ref[0, sl, :] = (qm[sl] * cqt + qs[sl] * sqt).astype(MXU_DTYPE)

    kvn = _rms(cols(384, 640), gkv_ref[...]).astype(MXU_DTYPE)
    kn = _nn(kvn, wk_ref[...])
    kpe = cols(640, 768) * ck_ref[...] + cols(768, 896) * sk_ref[...]
    for h in range(MLA_HEADS):
        sl = slice(LANES * h, LANES * (h + 1))
        ka_ref[0, :, sl] = (kn[:, sl] + kpe).astype(MXU_DTYPE)
    _store_values(vat_ref, _nt(wvt_ref[...], kvn))

    za = cols(896, 1408)
    za_ref[0] = (za * _sigmoid(za)).astype(MXU_DTYPE)
    kb_ref[0] = cols(1408, 1920).astype(MXU_DTYPE)
    zb = cols(1920, 2432)
    zb_ref[0] = (zb * _sigmoid(zb)).astype(MXU_DTYPE)
    ki_ref[0] = cols(2432, 2560).astype(MXU_DTYPE)
    ga_ref[0] = _sigmoid(cols(2560, 3584)).astype(MXU_DTYPE)
    gb_ref[0] = _sigmoid(cols(3584, 4608)).astype(MXU_DTYPE)

    qbt_ref[0] = (rows(0, 512) * (DSA_HEAD_DIM ** -0.5 * LOG2E)).astype(MXU_DTYPE)
    _store_values(vbt_ref, rows(512, 1024))
    qit_ref[0] = rows(1024, 1280).astype(MXU_DTYPE)
    wit_ref[0] = rows(1280, 1296)[:IDX_HEADS] * ((IDX_DIM * IDX_HEADS) ** -0.5)


def _diag_admissible():
    key = lax.broadcasted_iota(jnp.int32, (TILE, TILE), 0)
    qry = lax.broadcasted_iota(jnp.int32, (TILE, TILE), 1)
    return (key // CHUNK) <= (qry // CHUNK)


def _flash_init(m_ref, acc_ref):
    m_ref[...] = jnp.full(m_ref.shape, M_INIT, F32)
    acc_ref[...] = jnp.zeros(acc_ref.shape, F32)


class _Flash:
    def __init__(self, n, score, value, m_ref, mb_ref, acc_ref, s_refs, prepare=None):
        self.n, self.score, self.value, self.prepare = n, score, value, prepare
        self.m_ref, self.mb_ref, self.acc_ref, self.s_refs = m_ref, mb_ref, acc_ref, s_refs

    def put_scores(self, h, slot, s):
        self.s_refs[h][slot] = s
        mx = s[0:ROW_CHUNK]
        for c in range(1, TILE // ROW_CHUNK):
            mx = jnp.maximum(mx, s[ROW_CHUNK * c:ROW_CHUNK * (c + 1)])
        self.mb_ref[h, slot] = jnp.max(mx, axis=0, keepdims=True)

    def scores(self, j, slot, score=None):
        if self.prepare is not None:
            self.prepare(j, slot)
        for h in range(self.n):
            self.put_scores(h, slot, (score or self.score)(h, j, slot))

    def softmax_values(self, j, slot, nxt=None, nxt_score=None):
        rows = ROW_CHUNK
        n_chunks = TILE // rows
        if nxt is not None and self.prepare is not None:
            self.prepare(*nxt)
        for h in range(self.n):
            if nxt is not None:
                self.put_scores(h, nxt[1], (nxt_score or self.score)(h, *nxt))
            s_ref = self.s_refs[h]
            m_old = self.m_ref[h]
            m_new = jnp.maximum(m_old, self.mb_ref[h, slot])
            p = jnp.concatenate(
                [jnp.exp2(s_ref[slot, rows * c:rows * (c + 1)] - m_new).astype(MXU_DTYPE)
                 for c in range(n_chunks)], axis=0)
            self.m_ref[h] = m_new
            alpha = jnp.exp2(m_old - m_new)
            self.acc_ref[h] = alpha * self.acc_ref[h] + _nn(self.value(h, j), p)

    def pipeline(self, n_blocks, first_slot):
        _pipeline(n_blocks, lambda: None,
                  lambda j, slot: self.softmax_values(j, slot, nxt=(j + 1, 1 - slot)),
                  self.softmax_values, first_slot)


def _pipeline(n_blocks, first, step, last, first_slot=0):
    s0, s1 = first_slot, 1 - first_slot

    @pl.when(n_blocks > 0)
    def _():
        first()
        n_pairs = (n_blocks - 1) // 2

        def body(t, c):
            step(2 * t, s0)
            step(2 * t + 1, s1)
            return c

        lax.fori_loop(0, n_pairs, body, 0)
        j = 2 * n_pairs

        @pl.when(j + 1 < n_blocks)
        def _():
            step(j, s0)
            last(j + 1, s1)

        @pl.when(j + 1 == n_blocks)
        def _():
            last(j, s0)


def _flash_finish(n_heads, z_ref, o_ref, acc_ref, y_ref):
    for h in range(n_heads):
        y_ref[HEAD_V * h:HEAD_V * (h + 1), :] = acc_ref[h, 0:HEAD_V] / acc_ref[h, HEAD_V:HEAD_V + 1]
    o_ref[0] = (y_ref[...].T * z_ref[0].astype(F32)).astype(MXU_DTYPE)


def _mla_kernel(qt_ref, k_ref, vt_ref, z_ref, o_ref, m_ref, mb_ref, acc_ref, y_ref, *s_refs):
    i = pl.program_id(1)
    _flash_init(m_ref, acc_ref)

    def score(h, j, slot):
        sl = slice(LANES * h, LANES * (h + 1))
        off = pl.multiple_of(j * TILE, TILE)
        return _nn(k_ref[0, pl.ds(off, TILE), sl], qt_ref[0, sl, :])

    def diag_score(h, j, slot):
        return jnp.where(_diag_admissible(), score(h, j, slot), NEG_INF)

    def value(h, j):
        return vt_ref[0, j, V_ROWS * h:V_ROWS * (h + 1), :]

    flash = _Flash(MLA_HEADS, score, value, m_ref, mb_ref, acc_ref, s_refs)
    flash.scores(i, 0, diag_score)

    @pl.when(i == 0)
    def _():
        flash.softmax_values(i, 0)

    @pl.when(i >= 1)
    def _():
        flash.softmax_values(i, 0, nxt=(0, 1))
        flash.pipeline(i, first_slot=1)

    _flash_finish(MLA_HEADS, z_ref, o_ref, acc_ref, y_ref)


def _float_key(v):
    bits = lax.bitcast_convert_type(v, jnp.int32)
    return bits ^ ((bits >> 31) & 0x7FFFFFFF)


def _key_float(k):
    return lax.bitcast_convert_type(k ^ ((k >> 31) & 0x7FFFFFFF), F32)


def _top_half(v):
    bits = lax.bitcast_convert_type(v, jnp.int32) & -KEY_GRID
    return lax.bitcast_convert_type(bits, F32)


def _any(mask):
    return jnp.max(jnp.where(mask, 1.0, 0.0)) > 0.5


def _dsa_kernel(topk, qbt_ref, kb_ref, vbt_ref, qit_ref, ki_ref, wit_ref, zb_ref, bias_ref, o_ref,
                sc_ref, sc16_ref, qm_ref, qim_ref, sel_ref, lohi_ref, m_ref, mb_ref, acc_ref, y_ref,
                *s_refs):
    i = pl.program_id(1)
    adm = _diag_admissible()
    n_chunks = TILE // ROW_CHUNK

    row = lax.broadcasted_iota(jnp.int32, (LANES, TILE), 0)
    for h in range(DSA_HEADS):
        g = h // 2
        qm_ref[h] = jnp.where((row // DSA_HEAD_DIM) == (h % 2),
                              qbt_ref[0, LANES * g:LANES * (g + 1), :], 0).astype(MXU_DTYPE)
        gi = h // 4
        qim_ref[h] = jnp.where((row // IDX_DIM) == (h % 4),
                               qit_ref[0, LANES * gi:LANES * (gi + 1), :], 0).astype(MXU_DTYPE)

    def fold8(op, acc, v):
        for r in range(v.shape[0] // 8):
            acc = op(acc, v[8 * r:8 * (r + 1)])
        return acc

    heads_per_chunk = IDX_HEADS // n_chunks

    def stage_dots(j, slot, heads):
        off = pl.multiple_of(j * TILE, TILE)
        kij = ki_ref[0, pl.ds(off, TILE), :]
        for h in heads:
            s_refs[h][slot] = _nn(kij, qim_ref[h])

    def fold_scores(j, slot, masked=False, nxt=None):
        off = pl.multiple_of(j * TILE, TILE)
        lo8, hi8, pos8, nonneg8 = (lohi_ref[k] for k in range(4))
        for c in range(n_chunks):
            if nxt is not None:
                stage_dots(*nxt, range(heads_per_chunk * c, heads_per_chunk * (c + 1)))
            rs = slice(ROW_CHUNK * c, ROW_CHUNK * (c + 1))
            tot = jnp.zeros((ROW_CHUNK, TILE), F32)
            for h in range(IDX_HEADS):
                tot = tot + jnp.maximum(s_refs[h][slot, rs], 0.0) * wit_ref[0, h:h + 1, :]
            if masked:
                lo8 = fold8(jnp.minimum, lo8, jnp.where(adm[rs], tot, -NEG_INF))
                tot = jnp.where(adm[rs], tot, NEG_INF)
            else:
                lo8 = fold8(jnp.minimum, lo8, tot)
            hi8 = fold8(jnp.maximum, hi8, tot)
            pos8 = fold8(jnp.add, pos8, jnp.where(tot > 0.0, 1.0, 0.0))
            nonneg8 = fold8(jnp.add, nonneg8, jnp.where(tot >= 0.0, 1.0, 0.0))
            rows = pl.ds(pl.multiple_of(off + ROW_CHUNK * c, ROW_CHUNK), ROW_CHUNK)
            sc_ref[rows, :] = tot
            sc16_ref[rows, :] = _top_half(tot).astype(jnp.bfloat16)
        for k, v in enumerate((lo8, hi8, pos8, nonneg8)):
            lohi_ref[k] = v

    lohi_ref[0] = jnp.full((8, TILE), -NEG_INF, F32)
    lohi_ref[1] = jnp.full((8, TILE), NEG_INF, F32)
    lohi_ref[2] = jnp.zeros((8, TILE), F32)
    lohi_ref[3] = jnp.zeros((8, TILE), F32)
    stage_dots(i, 0, range(IDX_HEADS))
    fold_scores(i, 0, masked=True)
    _pipeline(i, lambda: stage_dots(0, 0, range(IDX_HEADS)),
              lambda j, slot: fold_scores(j, slot, nxt=(j + 1, 1 - slot)),
              fold_scores)
    smin = jnp.min(lohi_ref[0], axis=0, keepdims=True)
    smax = jnp.max(lohi_ref[1], axis=0, keepdims=True)

    def count(pred):
        def body(jb, acc):
            for c in range(n_chunks):
                off = pl.multiple_of(jb * TILE + c * ROW_CHUNK, ROW_CHUNK)
                acc = acc + jnp.where(pred(sc_ref[pl.ds(off, ROW_CHUNK), :], off), 1.0, 0.0)
            return acc
        acc = lax.fori_loop(0, i + 1, body, jnp.zeros((ROW_CHUNK, TILE), F32))
        return jnp.sum(acc, axis=0, keepdims=True)

    def mid_of(lo, hi):
        return (lo >> 1) + (hi >> 1) + (lo & hi & 1)

    kf = float(topk)
    qpos = i * TILE + lax.broadcasted_iota(jnp.int32, (1, TILE), 1)
    n_adm = ((qpos // CHUNK + 1) * CHUNK).astype(F32)
    few = n_adm <= kf

    def count16(thr16):
        one, zero = jnp.ones((), jnp.bfloat16), jnp.zeros((), jnp.bfloat16)

        def body(jb, acc):
            for c in range(n_chunks):
                off = pl.multiple_of(jb * TILE + c * ROW_CHUNK, ROW_CHUNK)
                acc = acc + jnp.where(sc16_ref[pl.ds(off, ROW_CHUNK), :] >= thr16, one, zero)
            return acc
        acc = lax.fori_loop(0, i + 1, body, jnp.zeros((ROW_CHUNK, TILE), jnp.bfloat16))
        return jnp.sum(acc.astype(F32), axis=0, keepdims=True)

    def probe_of(lo, hi):
        margin = (hi >> 3) - (lo >> 3)
        probe = _float_key(0.5 * _key_float(lo) + 0.5 * _key_float(hi))
        probe = jnp.clip(probe, lo + margin, hi - margin)
        return jnp.clip(probe, lo + 1, hi - 1)

    def grid_probe(lo, hi):
        near = (probe_of(lo, hi) + (KEY_GRID // 2)) & -KEY_GRID
        above_lo = ((lo >> 16) + 1) << 16
        probe = jnp.where(jnp.logical_and(near > lo, near < hi), near, above_lo)
        return probe, jnp.logical_and(mid_of(lo, hi) != lo, probe < hi)

    def update(state, probe, c, active):
        lo, hi, cnt, cnt_hi = state
        ge = c >= kf
        take_lo = jnp.logical_and(active, ge)
        take_hi = jnp.logical_and(active, jnp.logical_not(ge))
        new_hi = jnp.where(take_hi, probe, hi)
        new_hi = jnp.where(jnp.logical_and(active, c == kf), probe + 1, new_hi)
        return (jnp.where(take_lo, probe, lo), new_hi, jnp.where(take_lo, c, cnt),
                jnp.where(take_hi, c, cnt_hi))

    def coarse_pass(state):
        probe, active = grid_probe(state[0], state[1])
        thr16 = _top_half(_key_float(probe)).astype(jnp.bfloat16)
        return update(state, probe, count16(thr16), active)

    def open_(lo, hi):
        return mid_of(lo, hi) != lo

    def fine_pass(state):
        lo, hi = state[0], state[1]
        probe = probe_of(lo, hi)
        thr = _key_float(probe)
        return update(state, probe, count(lambda blk, off: blk >= thr), open_(lo, hi))

    def search(one_pass, is_active, state, per_check):
        def cond(c):
            return jnp.logical_and(c[0] < 128, _any(is_active(*c[1:])))

        def body(c):
            st = c[1:]
            for _ in range(per_check):
                st = one_pass(st)
            return (c[0] + per_check,) + st

        return lax.while_loop(cond, body, (jnp.int32(0),) + state)[1:]

    c_pos = jnp.sum(lohi_ref[2], axis=0, keepdims=True)
    c_nonneg = jnp.sum(lohi_ref[3], axis=0, keepdims=True)
    at_zero = jnp.logical_and(c_pos < kf, kf <= c_nonneg)
    above = kf <= c_pos
    lo0 = jnp.where(at_zero, 0, jnp.where(above, 1, _float_key(smin)))
    hi0 = jnp.where(at_zero, 1, jnp.where(above, _float_key(smax) + 1, -1))
    cnt0 = jnp.where(at_zero, c_nonneg, jnp.where(above, c_pos, n_adm))
    lo0 = jnp.where(few, KEY_LO, lo0)
    hi0 = jnp.where(few, KEY_LO + 1, hi0)
    cnt0 = jnp.where(few, n_adm, cnt0)
    cnt_hi0 = jnp.where(above, 0.0, c_nonneg)
    state = search(coarse_pass, lambda lo, hi, *_: grid_probe(lo, hi)[1], (lo0, hi0, cnt0, cnt_hi0), 4)
    state = search(fine_pass,
                   lambda lo, hi, cnt, cnt_hi: jnp.logical_and(open_(lo, hi), kf - cnt_hi > 1.0),
                   state, 2)
    lo, hi, cnt, _ = state
    last = open_(lo, hi)
    hi_f = _key_float(hi)

    def below_max(jb, acc):
        for c in range(n_chunks):
            off = pl.multiple_of(jb * TILE + c * ROW_CHUNK, ROW_CHUNK)
            blk = sc_ref[pl.ds(off, ROW_CHUNK), :]
            acc = jnp.maximum(acc, jnp.where(blk < hi_f, blk, NEG_INF))
        return acc

    def peel():
        top = lax.fori_loop(0, i + 1, below_max, jnp.full((ROW_CHUNK, TILE), NEG_INF, F32))
        thr_p = jnp.where(last, jnp.max(top, axis=0, keepdims=True), _key_float(lo))
        return thr_p, jnp.where(last, count(lambda blk, off: blk >= thr_p), cnt)

    thr, cnt = lax.cond(_any(last), peel, lambda: (_key_float(lo), cnt))

    excess = cnt > kf

    @pl.when(_any(excess))
    def _():
        need = kf - count(lambda blk, off: blk > thr)
        key = lax.broadcasted_iota(jnp.int32, (TILE, TILE), 0)
        upto = lax.broadcasted_iota(jnp.int32, (TILE, TILE), 1)
        prefix = jnp.where(upto <= key, 1.0, 0.0).astype(MXU_DTYPE)

        def drop(jb, seen):
            off = pl.multiple_of(jb * TILE, TILE)
            blk = sc_ref[pl.ds(off, TILE), :]
            tie = blk == thr
            rank = seen + _nn(prefix, jnp.where(tie, 1.0, 0.0).astype(MXU_DTYPE))
            sc_ref[pl.ds(off, TILE), :] = jnp.where(jnp.logical_and(tie, rank > need), NEG_INF, blk)
            return rank[TILE - 1:TILE, :]

        lax.fori_loop(0, i + 1, drop, jnp.zeros((1, TILE), F32))

    _flash_init(m_ref, acc_ref)

    def select(j, slot):
        off = pl.multiple_of(j * TILE, TILE)
        sel_ref[slot] = jnp.where(sc_ref[pl.ds(off, TILE), :] >= thr, 0.0, NEG_INF)

    def score(h, j, slot, near=None):
        g = h // 2
        off = pl.multiple_of(j * TILE, TILE)
        s = _nn(kb_ref[0, pl.ds(off, TILE), LANES * g:LANES * (g + 1)], qm_ref[h])
        if near is not None:
            s = s + bias_ref[h, near]
        return s + sel_ref[slot]

    def value(h, j):
        return vbt_ref[0, j, V_ROWS * h:V_ROWS * (h + 1), :]

    flash = _Flash(DSA_HEADS, score, value, m_ref, mb_ref, acc_ref, s_refs, prepare=select)
    flash.scores(i, 0, functools.partial(score, near=1))

    @pl.when(i == 0)
    def _():
        flash.softmax_values(i, 0)

    @pl.when(i >= 1)
    def _():
        flash.softmax_values(i, 0, nxt=(i - 1, 1), nxt_score=functools.partial(score, near=0))

        @pl.when(i == 1)
        def _():
            flash.softmax_values(i - 1, 1)

        @pl.when(i >= 2)
        def _():
            flash.softmax_values(i - 1, 1, nxt=(0, 0))
            flash.pipeline(i - 1, first_slot=0)

    _flash_finish(DSA_HEADS, zb_ref, o_ref, acc_ref, y_ref)


def _merge_kernel(x_ref, ya_ref, yb_ref, ga_ref, gb_ref, woa_ref, wob_ref, wout_ref, fg_ref, o_ref):
    merged = (ga_ref[0].astype(F32) * _nn(ya_ref[0], woa_ref[...])
              + gb_ref[0].astype(F32) * _nn(yb_ref[0], wob_ref[...]))
    y = x_ref[0] + _nn(merged.astype(MXU_DTYPE), wout_ref[...])
    o_ref[0] = _rms(y, fg_ref[...])


def _t5_bucket(rel):
    nb = REL_BUCKETS // 2
    max_exact = nb // 2
    ret = (rel > 0).astype(jnp.int32) * nb
    n = jnp.abs(rel)
    nf = jnp.maximum(n, 1).astype(jnp.float32)
    large = max_exact + (jnp.log(nf / max_exact) / np.log(REL_MAX_DIST / max_exact)
                         * (nb - max_exact)).astype(jnp.int32)
    large = jnp.minimum(large, nb - 1)
    return ret + jnp.where(n < max_exact, n, large)


def _bias_tables(rel_bias):
    key = jnp.arange(TILE, dtype=jnp.int32)[:, None]
    qry = jnp.arange(TILE, dtype=jnp.int32)[None, :]
    rel_diag = key - qry
    far = rel_bias[_t5_bucket(jnp.int32(-(TILE + 1)))]

    def table(rel):
        bucket = _t5_bucket(rel)[None]
        out = jnp.zeros((DSA_HEADS, TILE, TILE), F32)
        for b in range(REL_BUCKETS):
            out = jnp.where(bucket == b, rel_bias[b][:, None, None], out)
        return (out - far[:, None, None]) * LOG2E

    diag = jnp.where(((key // CHUNK) <= (qry // CHUNK))[None], table(rel_diag), NEG_INF)
    return jnp.stack([table(rel_diag - TILE), diag], axis=1).astype(F32)


def _rope_tables(seq):
    half = MLA_ROPE // 2
    freqs = ROPE_BASE ** (-jnp.arange(half, dtype=jnp.float32) / half)
    ang = jnp.arange(seq, dtype=jnp.int32).astype(jnp.float32)[:, None] * freqs[None, :]
    cos, sin = jnp.cos(ang), jnp.sin(ang)
    zeros = jnp.zeros((seq, LANES - MLA_NOPE - MLA_ROPE), F32)
    lead = jnp.zeros((seq, MLA_NOPE), F32)
    c_k = jnp.concatenate([lead, cos, cos, zeros], axis=1)
    s_k = jnp.concatenate([lead, -sin, sin, zeros], axis=1)
    scale = (MLA_NOPE + MLA_ROPE) ** -0.5 * LOG2E
    c_q = jnp.concatenate([lead + 1.0, cos, cos, zeros], axis=1) * scale
    return c_q.T, (s_k * scale).T, c_k, s_k


def _prep_weights(w_in, w_uq, w_ukv):
    cuts = np.cumsum(IN_SPLITS)[:-1].tolist()
    (w_qlat, w_ckv, w_kr, w_za, w_qb, w_kb, w_vb, w_zb,
     w_qi, w_ki, w_wi, w_ga, w_gb) = jnp.split(w_in, cuts, axis=1)
    half = MLA_ROPE // 2
    swap = np.concatenate([np.arange(half, MLA_ROPE), np.arange(half)])

    def z(n):
        return jnp.zeros((D_MODEL, n), F32)

    pad = LANES - MLA_NOPE - MLA_ROPE
    kr = jnp.concatenate([z(MLA_NOPE), w_kr, z(pad)], axis=1)
    krs = jnp.concatenate([z(MLA_NOPE), w_kr[:, swap], z(pad)], axis=1)
    ki4 = jnp.tile(w_ki, (1, LANES // IDX_DIM))
    w1 = jnp.concatenate([w_qlat, w_ckv, kr, krs, w_za, w_kb, w_zb, ki4, w_ga, w_gb], axis=1)
    w2t = jnp.concatenate([w_qb, w_vb, w_qi, w_wi, z(BF16_ROWS - IDX_HEADS)], axis=1).T

    wuq = w_uq.reshape(MLA_Q_RANK, MLA_HEADS, MLA_NOPE + MLA_ROPE)
    zq = jnp.zeros((MLA_Q_RANK, MLA_HEADS, pad), F32)
    wuqm = jnp.concatenate([wuq, zq], axis=2).reshape(MLA_Q_RANK, MLA_HEADS * LANES)
    wuqs = jnp.concatenate([jnp.zeros((MLA_Q_RANK, MLA_HEADS, MLA_NOPE), F32),
                            wuq[:, :, MLA_NOPE:][:, :, swap], zq], axis=2
                           ).reshape(MLA_Q_RANK, MLA_HEADS * LANES)
    wukv = w_ukv.reshape(MLA_KV_RANK, MLA_HEADS, MLA_NOPE + MLA_V)
    wk = jnp.concatenate([wukv[:, :, :MLA_NOPE],
                          jnp.zeros((MLA_KV_RANK, MLA_HEADS, LANES - MLA_NOPE), F32)], axis=2
                         ).reshape(MLA_KV_RANK, MLA_HEADS * LANES)
    wv = wukv[:, :, MLA_NOPE:].reshape(MLA_KV_RANK, MLA_WIDTH)
    c = lambda a: a.astype(MXU_DTYPE)
    return c(w1), c(w2t), c(wuqm.T), c(wuqs.T), c(wk), c(wv.T)


def _full(shape):
    return pl.BlockSpec(shape, lambda *_: (0,) * len(shape))


def _resident(shape, index_map):
    return pl.BlockSpec(shape, index_map, pipeline_mode=pl.Buffered(1))


def kernel(x, norm_g, w_in, g_q_lat, w_uq, g_kv_lat, w_ukv, w_o_a, w_o_b, w_out, rel_bias, final_g):
    B, S, D = x.shape
    assert D == D_MODEL and S % TILE == 0 and norm_g.shape[0] == 1
    assert MLA_V == HEAD_V and DSA_HEAD_DIM == HEAD_V
    assert S // ROW_CHUNK <= 256
    nb = S // TILE
    topk = min(TOPK_MAX, S // 4)
    bf = MXU_DTYPE
    params = functools.partial(pltpu.CompilerParams, vmem_limit_bytes=VMEM_LIMIT)

    w1, w2t, wuqm_t, wuqs_t, wk, wv_t = _prep_weights(w_in[0], w_uq[0], w_ukv[0])
    cq_t, sq_t, c_k, s_k = _rope_tables(S)
    n1, n2 = w1.shape[1], w2t.shape[0]
    va_rows, vb_rows = MLA_HEADS * V_ROWS, DSA_HEADS * V_ROWS

    tok = lambda c: pl.BlockSpec((1, TILE, c), lambda b, r: (b, r, 0))
    tok_t = lambda c: pl.BlockSpec((1, c, TILE), lambda b, r: (b, 0, r))
    blk_t = lambda c: pl.BlockSpec((1, 1, c, TILE), lambda b, r: (b, r, 0, 0))
    sds = jax.ShapeDtypeStruct
    (qa_t, ka, va_t, za, qb_t, kb, vb_t, zb, qi_t, ki, wi_t, ga, gb) = pl.pallas_call(
        _proj_kernel,
        grid=(B, nb),
        in_specs=[tok(D), _full((1, D)), _full((D, n1)), _full((n2, D)),
                  _full((1, MLA_Q_RANK)), _full((MLA_HEADS * LANES, MLA_Q_RANK)),
                  _full((MLA_HEADS * LANES, MLA_Q_RANK)),
                  _full((1, MLA_KV_RANK)), _full((MLA_KV_RANK, MLA_HEADS * LANES)),
                  _full((MLA_WIDTH, MLA_KV_RANK)),
                  pl.BlockSpec((LANES, TILE), lambda b, r: (0, r)),
                  pl.BlockSpec((LANES, TILE), lambda b, r: (0, r)),
                  pl.BlockSpec((TILE, LANES), lambda b, r: (r, 0)),
                  pl.BlockSpec((TILE, LANES), lambda b, r: (r, 0))],
        out_specs=[tok_t(MLA_HEADS * LANES), tok(MLA_HEADS * LANES), blk_t(va_rows), tok(MLA_WIDTH),
                   tok_t(DSA_WIDTH), tok(DSA_WIDTH), blk_t(vb_rows), tok(DSA_WIDTH),
                   tok_t(IDX_HEADS * IDX_DIM), tok(LANES), tok_t(IDX_HEADS), tok(D), tok(D)],
        out_shape=[sds((B, MLA_HEADS * LANES, S), bf), sds((B, S, MLA_HEADS * LANES), bf),
                   sds((B, nb, va_rows, TILE), bf), sds((B, S, MLA_WIDTH), bf),
                   sds((B, DSA_WIDTH, S), bf), sds((B, S, DSA_WIDTH), bf),
                   sds((B, nb, vb_rows, TILE), bf), sds((B, S, DSA_WIDTH), bf),
                   sds((B, IDX_HEADS * IDX_DIM, S), bf), sds((B, S, LANES), bf),
                   sds((B, IDX_HEADS, S), F32), sds((B, S, D), bf), sds((B, S, D), bf)],
        compiler_params=params(dimension_semantics=("parallel", "parallel")),
        name="proj",
    )(x, norm_g, w1, w2t, g_q_lat, wuqm_t, wuqs_t, g_kv_lat, wk, wv_t, cq_t, sq_t, c_k, s_k)

    def flash_state(heads):
        return ([pltpu.VMEM((heads, 1, TILE), F32),
                 pltpu.VMEM((heads, 2, 1, TILE), F32),
                 pltpu.VMEM((heads, V_ROWS, TILE), F32),
                 pltpu.VMEM((heads * HEAD_V, TILE), F32)]
                + [pltpu.VMEM((2, TILE, TILE), F32)] * heads)

    ya = pl.pallas_call(
        _mla_kernel,
        grid=(B, nb),
        in_specs=[pl.BlockSpec((1, MLA_HEADS * LANES, TILE), lambda b, i: (b, 0, i)),
                  _resident((1, S, MLA_HEADS * LANES), lambda b, i: (b, 0, 0)),
                  _resident((1, nb, va_rows, TILE), lambda b, i: (b, 0, 0, 0)),
                  pl.BlockSpec((1, TILE, MLA_WIDTH), lambda b, i: (b, i, 0))],
        out_specs=pl.BlockSpec((1, TILE, MLA_WIDTH), lambda b, i: (b, i, 0)),
        out_shape=sds((B, S, MLA_WIDTH), bf),
        scratch_shapes=flash_state(MLA_HEADS),
        compiler_params=params(dimension_semantics=("parallel", "arbitrary")),
        name="mla",
    )(qa_t, ka, va_t, za)

    yb = pl.pallas_call(
        functools.partial(_dsa_kernel, topk),
        grid=(B, nb),
        in_specs=[pl.BlockSpec((1, DSA_WIDTH, TILE), lambda b, i: (b, 0, i)),
                  _resident((1, S, DSA_WIDTH), lambda b, i: (b, 0, 0)),
                  _resident((1, nb, vb_rows, TILE), lambda b, i: (b, 0, 0, 0)),
                  pl.BlockSpec((1, IDX_HEADS * IDX_DIM, TILE), lambda b, i: (b, 0, i)),
                  _resident((1, S, LANES), lambda b, i: (b, 0, 0)),
                  pl.BlockSpec((1, IDX_HEADS, TILE), lambda b, i: (b, 0, i)),
                  pl.BlockSpec((1, TILE, DSA_WIDTH), lambda b, i: (b, i, 0)),
                  _resident((DSA_HEADS, 2, TILE, TILE), lambda b, i: (0, 0, 0, 0))],
        out_specs=pl.BlockSpec((1, TILE, DSA_WIDTH), lambda b, i: (b, i, 0)),
        out_shape=sds((B, S, DSA_WIDTH), bf),
        scratch_shapes=[pltpu.VMEM((S, TILE), F32),
                        pltpu.VMEM((S, TILE), jnp.bfloat16),
                        pltpu.VMEM((DSA_HEADS, LANES, TILE), bf),
                        pltpu.VMEM((IDX_HEADS, LANES, TILE), bf),
                        pltpu.VMEM((2, TILE, TILE), F32),
                        pltpu.VMEM((4, 8, TILE), F32),
                        *flash_state(DSA_HEADS)],
        compiler_params=params(dimension_semantics=("parallel", "arbitrary")),
        name="dsa",
    )(qb_t, kb, vb_t, qi_t, ki, wi_t, zb, _bias_tables(rel_bias))

    return pl.pallas_call(
        _merge_kernel,
        grid=(B, nb),
        in_specs=[tok(D), tok(MLA_WIDTH), tok(DSA_WIDTH), tok(D), tok(D),
                  _full((MLA_WIDTH, D)), _full((DSA_WIDTH, D)), _full((D, D)), _full((1, D))],
        out_specs=tok(D),
        out_shape=sds((B, S, D), x.dtype),
        compiler_params=params(dimension_semantics=("parallel", "parallel")),
        name="merge",
    )(x, ya, yb, ga, gb, w_o_a[0].astype(bf), w_o_b[0].astype(bf), w_out[0].astype(bf),
      final_g.reshape(1, D))
```

```python
import functools
import math

import numpy as np
import jax
import jax.numpy as jnp
from jax import lax
from jax.experimental import pallas as pl
from jax.experimental.pallas import tpu as pltpu

D_MODEL = 1024
CHUNK = 64
EPS = 1e-6
MLA_HEADS = 8
MLA_NOPE = 64
MLA_ROPE = 32
MLA_V = 64
MLA_Q_RANK = 384
MLA_KV_RANK = 256
ROPE_BASE = 10000.0
MLA_WIDTH = MLA_HEADS * MLA_V
DSA_HEADS = 8
DSA_HEAD_DIM = 64
DSA_WIDTH = DSA_HEADS * DSA_HEAD_DIM
IDX_HEADS = 8
IDX_DIM = 32
TOPK_MAX = 256
REL_BUCKETS = 32
REL_MAX_DIST = 128
IN_SPLITS = (MLA_Q_RANK, MLA_KV_RANK, MLA_ROPE, MLA_WIDTH,
             DSA_WIDTH, DSA_WIDTH, DSA_WIDTH, DSA_WIDTH,
             IDX_HEADS * IDX_DIM, IDX_DIM, IDX_HEADS,
             D_MODEL, D_MODEL)

LANES = 128
BF16_ROWS = 16
TILE = 256
ROW_CHUNK = 64
HEAD_V = 64
V_ROWS = HEAD_V + BF16_ROWS
VMEM_LIMIT = 56 * 1024 * 1024
MXU_DTYPE = jnp.bfloat16
F32 = jnp.float32
NEG_INF = float("-inf")
M_INIT = -1e30
LOG2E = math.log2(math.e)

KEY_LO = int(np.int32(np.uint32(0x80800000)))
KEY_GRID = 1 << 16


def _nt(a, b):
    return lax.dot_general(a, b, (((1,), (1,)), ((), ())), preferred_element_type=F32)


def _nn(a, b):
    return jnp.dot(a, b, preferred_element_type=F32)


def _sigmoid(v):
    return 1.0 / (1.0 + jnp.exp(-v))


def _rms(v, g):
    return v * lax.rsqrt(jnp.mean(v * v, axis=-1, keepdims=True) + EPS) * g


def _store_values(vt_ref, v_t):
    row = lax.broadcasted_iota(jnp.int32, (BF16_ROWS, TILE), 0)
    ones_row = jnp.where(row == 0, 1.0, 0.0).astype(MXU_DTYPE)
    for h in range(v_t.shape[0] // HEAD_V):
        vt_ref[0, 0, V_ROWS * h:V_ROWS * h + HEAD_V, :] = v_t[HEAD_V * h:HEAD_V * (h + 1)].astype(MXU_DTYPE)
        vt_ref[0, 0, V_ROWS * h + HEAD_V:V_ROWS * (h + 1), :] = ones_row


def _proj_kernel(x_ref, ng_ref, w1_ref, w2t_ref, gq_ref, wuqm_ref, wuqs_ref, gkv_ref, wk_ref,
                 wvt_ref, cqt_ref, sqt_ref, ck_ref, sk_ref,
                 qat_ref, ka_ref, vat_ref, za_ref, qbt_ref, kb_ref, vbt_ref, zb_ref,
                 qit_ref, ki_ref, wit_ref, ga_ref, gb_ref):
    hb = _rms(x_ref[0], ng_ref[...]).astype(MXU_DTYPE)

    def cols(lo, hi):
        return _nn(hb, w1_ref[:, lo:hi])

    def rows(lo, hi):
        return _nt(w2t_ref[lo:hi, :], hb)

    qn = _rms(cols(0, 384), gq_ref[...]).astype(MXU_DTYPE)
    qm = _nt(wuqm_ref[...], qn)
    qs = _nt(wuqs_ref[...], qn)
    cqt = cqt_ref[...]
    sqt = sqt_ref[...]
    for h in range(MLA_HEADS):
        sl = slice(LANES * h, LANES * (h + 1))
        qat_ref[0, sl, :] = (qm[sl] * cqt + qs[sl] * sqt).astype(MXU_DTYPE)

    kvn = _rms(cols(384, 640), gkv_ref[...]).astype(MXU_DTYPE)
    kn = _nn(kvn, wk_ref[...])
    kpe = cols(640, 768) * ck_ref[...] + cols(768, 896) * sk_ref[...]
    for h in range(MLA_HEADS):
        sl = slice(LANES * h, LANES * (h + 1))
        ka_ref[0, :, sl] = (kn[:, sl] + kpe).astype(MXU_DTYPE)
    _store_values(vat_ref, _nt(wvt_ref[...], kvn))

    za = cols(896, 1408)
    za_ref[0] = (za * _sigmoid(za)).astype(MXU_DTYPE)
    kb_ref[0] = cols(1408, 1920).astype(MXU_DTYPE)
    zb = cols(1920, 2432)
    zb_ref[0] = (zb * _sigmoid(zb)).astype(MXU_DTYPE)
    ki_ref[0] = cols(2432, 2560).astype(MXU_DTYPE)
    ga_ref[0] = _sigmoid(cols(2560, 3584)).astype(MXU_DTYPE)
    gb_ref[0] = _sigmoid(cols(3584, 4608)).astype(MXU_DTYPE)

    qbt_ref[0] = (rows(0, 512) * (DSA_HEAD_DIM ** -0.5 * LOG2E)).astype(MXU_DTYPE)
    _store_values(vbt_ref, rows(512, 1024))
    qit_ref[0] = rows(1024, 1280).astype(MXU_DTYPE)
    wit_ref[0] = rows(1280, 1296)[:IDX_HEADS] * ((IDX_DIM * IDX_HEADS) ** -0.5)


def _diag_admissible():
    key = lax.broadcasted_iota(jnp.int32, (TILE, TILE), 0)
    qry = lax.broadcasted_iota(jnp.int32, (TILE, TILE), 1)
    return (key // CHUNK) <= (qry // CHUNK)


def _flash_init(m_ref, acc_ref):
    m_ref[...] = jnp.full(m_ref.shape, M_INIT, F32)
    acc_ref[...] = jnp.zeros(acc_ref.shape, F32)


class _Flash:
    def __init__(self, n, score, value, m_ref, mb_ref, acc_ref, s_refs, prepare=None):
        self.n, self.score, self.value, self.prepare = n, score, value, prepare
        self.m_ref, self.mb_ref, self.acc_ref, self.s_refs = m_ref, mb_ref, acc_ref, s_refs

    def put_scores(self, h, slot, s):
        self.s_refs[h][slot] = s
        mx = s[0:ROW_CHUNK]
        for c in range(1, TILE // ROW_CHUNK):
            mx = jnp.maximum(mx, s[ROW_CHUNK * c:ROW_CHUNK * (c + 1)])
        self.mb_ref[h, slot] = jnp.max(mx, axis=0, keepdims=True)

    def scores(self, j, slot, score=None):
        if self.prepare is not None:
            self.prepare(j, slot)
        for h in range(self.n):
            self.put_scores(h, slot, (score or self.score)(h, j, slot))

    def softmax_values(self, j, slot, nxt=None, nxt_score=None):
        rows = ROW_CHUNK
        n_chunks = TILE // rows
        if nxt is not None and self.prepare is not None:
            self.prepare(*nxt)
        for h in range(self.n):
            if nxt is not None:
                self.put_scores(h, nxt[1], (nxt_score or self.score)(h, *nxt))
            s_ref = self.s_refs[h]
            m_old = self.m_ref[h]
            m_new = jnp.maximum(m_old, self.mb_ref[h, slot])
            p = jnp.concatenate(
                [jnp.exp2(s_ref[slot, rows * c:rows * (c + 1)] - m_new).astype(MXU_DTYPE)
                 for c in range(n_chunks)], axis=0)
            self.m_ref[h] = m_new
            alpha = jnp.exp2(m_old - m_new)
            self.acc_ref[h] = alpha * self.acc_ref[h] + _nn(self.value(h, j), p)

    def pipeline(self, n_blocks, first_slot):
        _pipeline(n_blocks, lambda: None,
                  lambda j, slot: self.softmax_values(j, slot, nxt=(j + 1, 1 - slot)),
                  self.softmax_values, first_slot)


def _pipeline(n_blocks, first, step, last, first_slot=0):
    s0, s1 = first_slot, 1 - first_slot

    @pl.when(n_blocks > 0)
    def _():
        first()
        n_pairs = (n_blocks - 1) // 2

        def body(t, c):
            step(2 * t, s0)
            step(2 * t + 1, s1)
            return c

        lax.fori_loop(0, n_pairs, body, 0)
        j = 2 * n_pairs

        @pl.when(j + 1 < n_blocks)
        def _():
            step(j, s0)
            last(j + 1, s1)

        @pl.when(j + 1 == n_blocks)
        def _():
            last(j, s0)


def _flash_finish(n_heads, z_ref, o_ref, acc_ref, y_ref):
    for h in range(n_heads):
        y_ref[HEAD_V * h:HEAD_V * (h + 1), :] = acc_ref[h, 0:HEAD_V] / acc_ref[h, HEAD_V:HEAD_V + 1]
    o_ref[0] = (y_ref[...].T * z_ref[0].astype(F32)).astype(MXU_DTYPE)


def _mla_kernel(qt_ref, k_ref, vt_ref, z_ref, o_ref, m_ref, mb_ref, acc_ref, y_ref, *s_refs):
    i = pl.program_id(1)
    _flash_init(m_ref, acc_ref)

    def score(h, j, slot):
        sl = slice(LANES * h, LANES * (h + 1))
        off = pl.multiple_of(j * TILE, TILE)
        return _nn(k_ref[0, pl.ds(off, TILE), sl], qt_ref[0, sl, :])

    def diag_score(h, j, slot):
        return jnp.where(_diag_admissible(), score(h, j, slot), NEG_INF)

    def value(h, j):
        return vt_ref[0, j, V_ROWS * h:V_ROWS * (h + 1), :]

    flash = _Flash(MLA_HEADS, score, value, m_ref, mb_ref, acc_ref, s_refs)
    flash.scores(i, 0, diag_score)

    @pl.when(i == 0)
    def _():
        flash.softmax_values(i, 0)

    @pl.when(i >= 1)
    def _():
        flash.softmax_values(i, 0, nxt=(0, 1))
        flash.pipeline(i, first_slot=1)

    _flash_finish(MLA_HEADS, z_ref, o_ref, acc_ref, y_ref)


def _float_key(v):
    bits = lax.bitcast_convert_type(v, jnp.int32)
    return bits ^ ((bits >> 31) & 0x7FFFFFFF)


def _key_float(k):
    return lax.bitcast_convert_type(k ^ ((k >> 31) & 0x7FFFFFFF), F32)


def _top_half(v):
    bits = lax.bitcast_convert_type(v, jnp.int32) & -KEY_GRID
    return lax.bitcast_convert_type(bits, F32)


def _any(mask):
    return jnp.max(jnp.where(mask, 1.0, 0.0)) > 0.5


def _dsa_kernel(topk, qbt_ref, kb_ref, vbt_ref, qit_ref, ki_ref, wit_ref, zb_ref, bias_ref, o_ref,
                sc_ref, sc16_ref, qm_ref, qim_ref, sel_ref, lohi_ref, m_ref, mb_ref, acc_ref, y_ref,
                *s_refs):
    i = pl.program_id(1)
    adm = _diag_admissible()
    n_chunks = TILE // ROW_CHUNK

    row = lax.broadcasted_iota(jnp.int32, (LANES, TILE), 0)
    for h in range(DSA_HEADS):
        g = h // 2
        qm_ref[h] = jnp.where((row // DSA_HEAD_DIM) == (h % 2),
                              qbt_ref[0, LANES * g:LANES * (g + 1), :], 0).astype(MXU_DTYPE)
        gi = h // 4
        qim_ref[h] = jnp.where((row // IDX_DIM) == (h % 4),
                               qit_ref[0, LANES * gi:LANES * (gi + 1), :], 0).astype(MXU_DTYPE)

    def fold8(op, acc, v):
        for r in range(v.shape[0] // 8):
            acc = op(acc, v[8 * r:8 * (r + 1)])
        return acc

    heads_per_chunk = IDX_HEADS // n_chunks

    def stage_dots(j, slot, heads):
        off = pl.multiple_of(j * TILE, TILE)
        kij = ki_ref[0, pl.ds(off, TILE), :]
        for h in heads:
            s_refs[h][slot] = _nn(kij, qim_ref[h])

    def fold_scores(j, slot, masked=False, nxt=None):
        off = pl.multiple_of(j * TILE, TILE)
        lo8, hi8, pos8, nonneg8 = (lohi_ref[k] for k in range(4))
        for c in range(n_chunks):
            if nxt is not None:
                stage_dots(*nxt, range(heads_per_chunk * c, heads_per_chunk * (c + 1)))
            rs = slice(ROW_CHUNK * c, ROW_CHUNK * (c + 1))
            tot = jnp.zeros((ROW_CHUNK, TILE), F32)
            for h in range(IDX_HEADS):
                tot = tot + jnp.maximum(s_refs[h][slot, rs], 0.0) * wit_ref[0, h:h + 1, :]
            if masked:
                lo8 = fold8(jnp.minimum, lo8, jnp.where(adm[rs], tot, -NEG_INF))
                tot = jnp.where(adm[rs], tot, NEG_INF)
            else:
                lo8 = fold8(jnp.minimum, lo8, tot)
            hi8 = fold8(jnp.maximum, hi8, tot)
            pos8 = fold8(jnp.add, pos8, jnp.where(tot > 0.0, 1.0, 0.0))
            nonneg8 = fold8(jnp.add, nonneg8, jnp.where(tot >= 0.0, 1.0, 0.0))
            rows = pl.ds(pl.multiple_of(off + ROW_CHUNK * c, ROW_CHUNK), ROW_CHUNK)
            sc_ref[rows, :] = tot
            sc16_ref[rows, :] = _top_half(tot).astype(jnp.bfloat16)
        for k, v in enumerate((lo8, hi8, pos8, nonneg8)):
            lohi_ref[k] = v

    lohi_ref[0] = jnp.full((8, TILE), -NEG_INF, F32)
    lohi_ref[1] = jnp.full((8, TILE), NEG_INF, F32)
    lohi_ref[2] = jnp.zeros((8, TILE), F32)
    lohi_ref[3] = jnp.zeros((8, TILE), F32)
    stage_dots(i, 0, range(IDX_HEADS))
    fold_scores(i, 0, masked=True)
    _pipeline(i, lambda: stage_dots(0, 0, range(IDX_HEADS)),
              lambda j, slot: fold_scores(j, slot, nxt=(j + 1, 1 - slot)),
              fold_scores)
    smin = jnp.min(lohi_ref[0], axis=0, keepdims=True)
    smax = jnp.max(lohi_ref[1], axis=0, keepdims=True)

    def count(pred):
        def body(jb, acc):
            for c in range(n_chunks):
                off = pl.multiple_of(jb * TILE + c * ROW_CHUNK, ROW_CHUNK)
                acc = acc + jnp.where(pred(sc_ref[pl.ds(off, ROW_CHUNK), :], off), 1.0, 0.0)
            return acc
        acc = lax.fori_loop(0, i + 1, body, jnp.zeros((ROW_CHUNK, TILE), F32))
        return jnp.sum(acc, axis=0, keepdims=True)

    def mid_of(lo, hi):
        return (lo >> 1) + (hi >> 1) + (lo & hi & 1)

    kf = float(topk)
    qpos = i * TILE + lax.broadcasted_iota(jnp.int32, (1, TILE), 1)
    n_adm = ((qpos // CHUNK + 1) * CHUNK).astype(F32)
    few = n_adm <= kf

    def count16(thr16):
        one, zero = jnp.ones((), jnp.bfloat16), jnp.zeros((), jnp.bfloat16)

        def body(jb, acc):
            for c in range(n_chunks):
                off = pl.multiple_of(jb * TILE + c * ROW_CHUNK, ROW_CHUNK)
                acc = acc + jnp.where(sc16_ref[pl.ds(off, ROW_CHUNK), :] >= thr16, one, zero)
            return acc
        acc = lax.fori_loop(0, i + 1, body, jnp.zeros((ROW_CHUNK, TILE), jnp.bfloat16))
        return jnp.sum(acc.astype(F32), axis=0, keepdims=True)

    def probe_of(lo, hi):
        margin = (hi >> 3) - (lo >> 3)
        probe = _float_key(0.5 * _key_float(lo) + 0.5 * _key_float(hi))
        probe = jnp.clip(probe, lo + margin, hi - margin)
        return jnp.clip(probe, lo + 1, hi - 1)

    def grid_probe(lo, hi):
        near = (probe_of(lo, hi) + (KEY_GRID // 2)) & -KEY_GRID
        above_lo = ((lo >> 16) + 1) << 16
        probe = jnp.where(jnp.logical_and(near > lo, near < hi), near, above_lo)
        return probe, jnp.logical_and(mid_of(lo, hi) != lo, probe < hi)

    def update(state, probe, c, active):
        lo, hi, cnt, cnt_hi = state
        ge = c >= kf
        take_lo = jnp.logical_and(active, ge)
        take_hi = jnp.logical_and(active, jnp.logical_not(ge))
        new_hi = jnp.where(take_hi, probe, hi)
        new_hi = jnp.where(jnp.logical_and(active, c == kf), probe + 1, new_hi)
        return (jnp.where(take_lo, probe, lo), new_hi, jnp.where(take_lo, c, cnt),
                jnp.where(take_hi, c, cnt_hi))

    def coarse_pass(state):
        probe, active = grid_probe(state[0], state[1])
        thr16 = _top_half(_key_float(probe)).astype(jnp.bfloat16)
        return update(state, probe, count16(thr16), active)

    def open_(lo, hi):
        return mid_of(lo, hi) != lo

    def fine_pass(state):
        lo, hi = state[0], state[1]
        probe = probe_of(lo, hi)
        thr = _key_float(probe)
        return update(state, probe, count(lambda blk, off: blk >= thr), open_(lo, hi))

    def search(one_pass, is_active, state, per_check):
        def cond(c):
            return jnp.logical_and(c[0] < 128, _any(is_active(*c[1:])))

        def body(c):
            st = c[1:]
            for _ in range(per_check):
                st = one_pass(st)
            return (c[0] + per_check,) + st

        return lax.while_loop(cond, body, (jnp.int32(0),) + state)[1:]

    c_pos = jnp.sum(lohi_ref[2], axis=0, keepdims=True)
    c_nonneg = jnp.sum(lohi_ref[3], axis=0, keepdims=True)
    at_zero = jnp.logical_and(c_pos < kf, kf <= c_nonneg)
    above = kf <= c_pos
    lo0 = jnp.where(at_zero, 0, jnp.where(above, 1, _float_key(smin)))
    hi0 = jnp.where(at_zero, 1, jnp.where(above, _float_key(smax) + 1, -1))
    cnt0 = jnp.where(at_zero, c_nonneg, jnp.where(above, c_pos, n_adm))
    lo0 = jnp.where(few, KEY_LO, lo0)
    hi0 = jnp.where(few, KEY_LO + 1, hi0)
    cnt0 = jnp.where(few, n_adm, cnt0)
    cnt_hi0 = jnp.where(above, 0.0, c_nonneg)
    state = search(coarse_pass, lambda lo, hi, *_: grid_probe(lo, hi)[1], (lo0, hi0, cnt0, cnt_hi0), 2)
    state = search(fine_pass,
                   lambda lo, hi, cnt, cnt_hi: jnp.logical_and(open_(lo, hi), kf - cnt_hi > 1.0),
                   state, 2)
    lo, hi, cnt, _ = state
    last = open_(lo, hi)
    hi_f = _key_float(hi)

    def below_max(jb, acc):
        for c in range(n_chunks):
            off = pl.multiple_of(jb * TILE + c * ROW_CHUNK, ROW_CHUNK)
            blk = sc_ref[pl.ds(off, ROW_CHUNK), :]
            acc = jnp.maximum(acc, jnp.where(blk < hi_f, blk, NEG_INF))
        return acc

    def peel():
        top = lax.fori_loop(0, i + 1, below_max, jnp.full((ROW_CHUNK, TILE), NEG_INF, F32))
        thr_p = jnp.where(last, jnp.max(top, axis=0, keepdims=True), _key_float(lo))
        return thr_p, jnp.where(last, count(lambda blk, off: blk >= thr_p), cnt)

    thr, cnt = lax.cond(_any(last), peel, lambda: (_key_float(lo), cnt))

    excess = cnt > kf

    @pl.when(_any(excess))
    def _():
        need = kf - count(lambda blk, off: blk > thr)
        key = lax.broadcasted_iota(jnp.int32, (TILE, TILE), 0)
        upto = lax.broadcasted_iota(jnp.int32, (TILE, TILE), 1)
        prefix = jnp.where(upto <= key, 1.0, 0.0).astype(MXU_DTYPE)

        def drop(jb, seen):
            off = pl.multiple_of(jb * TILE, TILE)
            blk = sc_ref[pl.ds(off, TILE), :]
            tie = blk == thr
            tie01 = jnp.where(tie, 1.0, 0.0)
            rank = seen + _nn(prefix, tie01.astype(MXU_DTYPE))
            sc_ref[pl.ds(off, TILE), :] = jnp.where(jnp.logical_and(tie, rank > need), NEG_INF, blk)
            return seen + jnp.sum(tie01, axis=0, keepdims=True)

        lax.fori_loop(0, i + 1, drop, jnp.zeros((1, TILE), F32))

    _flash_init(m_ref, acc_ref)

    def select(j, slot):
        off = pl.multiple_of(j * TILE, TILE)
        sel_ref[slot] = jnp.where(sc_ref[pl.ds(off, TILE), :] >= thr, 0.0, NEG_INF)

    def score(h, j, slot, near=None):
        g = h // 2
        off = pl.multiple_of(j * TILE, TILE)
        s = _nn(kb_ref[0, pl.ds(off, TILE), LANES * g:LANES * (g + 1)], qm_ref[h])
        if near is not None:
            s = s + bias_ref[h, near]
        return s + sel_ref[slot]

    def value(h, j):
        return vbt_ref[0, j, V_ROWS * h:V_ROWS * (h + 1), :]

    flash = _Flash(DSA_HEADS, score, value, m_ref, mb_ref, acc_ref, s_refs, prepare=select)
    flash.scores(i, 0, functools.partial(score, near=1))

    @pl.when(i == 0)
    def _():
        flash.softmax_values(i, 0)

    @pl.when(i >= 1)
    def _():
        flash.softmax_values(i, 0, nxt=(i - 1, 1), nxt_score=functools.partial(score, near=0))

        @pl.when(i == 1)
        def _():
            flash.softmax_values(i - 1, 1)

        @pl.when(i >= 2)
        def _():
            flash.softmax_values(i - 1, 1, nxt=(0, 0))
            flash.pipeline(i - 1, first_slot=0)

    _flash_finish(DSA_HEADS, zb_ref, o_ref, acc_ref, y_ref)


def _merge_kernel(x_ref, ya_ref, yb_ref, ga_ref, gb_ref, woa_ref, wob_ref, wout_ref, fg_ref, o_ref):
    merged = (ga_ref[0].astype(F32) * _nn(ya_ref[0], woa_ref[...])
              + gb_ref[0].astype(F32) * _nn(yb_ref[0], wob_ref[...]))
    y = x_ref[0] + _nn(merged.astype(MXU_DTYPE), wout_ref[...])
    o_ref[0] = _rms(y, fg_ref[...])


def _t5_bucket(rel):
    nb = REL_BUCKETS // 2
    max_exact = nb // 2
    ret = (rel > 0).astype(jnp.int32) * nb
    n = jnp.abs(rel)
    nf = jnp.maximum(n, 1).astype(jnp.float32)
    large = max_exact + (jnp.log(nf / max_exact) / np.log(REL_MAX_DIST / max_exact)
                         * (nb - max_exact)).astype(jnp.int32)
    large = jnp.minimum(large, nb - 1)
    return ret + jnp.where(n < max_exact, n, large)


def _bias_tables(rel_bias):
    key = jnp.arange(TILE, dtype=jnp.int32)[:, None]
    qry = jnp.arange(TILE, dtype=jnp.int32)[None, :]
    rel_diag = key - qry
    far = rel_bias[_t5_bucket(jnp.int32(-(TILE + 1)))]

    def table(rel):
        bucket = _t5_bucket(rel)[None]
        out = jnp.zeros((DSA_HEADS, TILE, TILE), F32)
        for b in range(REL_BUCKETS):
            out = jnp.where(bucket == b, rel_bias[b][:, None, None], out)
        return (out - far[:, None, None]) * LOG2E

    diag = jnp.where(((key // CHUNK) <= (qry // CHUNK))[None], table(rel_diag), NEG_INF)
    return jnp.stack([table(rel_diag - TILE), diag], axis=1).astype(F32)


def _rope_tables(seq):
    half = MLA_ROPE // 2
    freqs = ROPE_BASE ** (-jnp.arange(half, dtype=jnp.float32) / half)
    ang = jnp.arange(seq, dtype=jnp.int32).astype(jnp.float32)[:, None] * freqs[None, :]
    cos, sin = jnp.cos(ang), jnp.sin(ang)
    zeros = jnp.zeros((seq, LANES - MLA_NOPE - MLA_ROPE), F32)
    lead = jnp.zeros((seq, MLA_NOPE), F32)
    c_k = jnp.concatenate([lead, cos, cos, zeros], axis=1)
    s_k = jnp.concatenate([lead, -sin, sin, zeros], axis=1)
    scale = (MLA_NOPE + MLA_ROPE) ** -0.5 * LOG2E
    c_q = jnp.concatenate([lead + 1.0, cos, cos, zeros], axis=1) * scale
    return c_q.T, (s_k * scale).T, c_k, s_k


def _prep_weights(w_in, w_uq, w_ukv):
    cuts = np.cumsum(IN_SPLITS)[:-1].tolist()
    (w_qlat, w_ckv, w_kr, w_za, w_qb, w_kb, w_vb, w_zb,
     w_qi, w_ki, w_wi, w_ga, w_gb) = jnp.split(w_in, cuts, axis=1)
    half = MLA_ROPE // 2
    swap = np.concatenate([np.arange(half, MLA_ROPE), np.arange(half)])

    def z(n):
        return jnp.zeros((D_MODEL, n), F32)

    pad = LANES - MLA_NOPE - MLA_ROPE
    kr = jnp.concatenate([z(MLA_NOPE), w_kr, z(pad)], axis=1)
    krs = jnp.concatenate([z(MLA_NOPE), w_kr[:, swap], z(pad)], axis=1)
    ki4 = jnp.tile(w_ki, (1, LANES // IDX_DIM))
    w1 = jnp.concatenate([w_qlat, w_ckv, kr, krs, w_za, w_kb, w_zb, ki4, w_ga, w_gb], axis=1)
    w2t = jnp.concatenate([w_qb, w_vb, w_qi, w_wi, z(BF16_ROWS - IDX_HEADS)], axis=1).T

    wuq = w_uq.reshape(MLA_Q_RANK, MLA_HEADS, MLA_NOPE + MLA_ROPE)
    zq = jnp.zeros((MLA_Q_RANK, MLA_HEADS, pad), F32)
    wuqm = jnp.concatenate([wuq, zq], axis=2).reshape(MLA_Q_RANK, MLA_HEADS * LANES)
    wuqs = jnp.concatenate([jnp.zeros((MLA_Q_RANK, MLA_HEADS, MLA_NOPE), F32),
                            wuq[:, :, MLA_NOPE:][:, :, swap], zq], axis=2
                           ).reshape(MLA_Q_RANK, MLA_HEADS * LANES)
    wukv = w_ukv.reshape(MLA_KV_RANK, MLA_HEADS, MLA_NOPE + MLA_V)
    wk = jnp.concatenate([wukv[:, :, :MLA_NOPE],
                          jnp.zeros((MLA_KV_RANK, MLA_HEADS, LANES - MLA_NOPE), F32)], axis=2
                         ).reshape(MLA_KV_RANK, MLA_HEADS * LANES)
    wv = wukv[:, :, MLA_NOPE:].reshape(MLA_KV_RANK, MLA_WIDTH)
    c = lambda a: a.astype(MXU_DTYPE)
    return c(w1), c(w2t), c(wuqm.T), c(wuqs.T), c(wk), c(wv.T)


def _full(shape):
    return pl.BlockSpec(shape, lambda *_: (0,) * len(shape))


def _resident(shape, index_map):
    return pl.BlockSpec(shape, index_map, pipeline_mode=pl.Buffered(1))


def kernel(x, norm_g, w_in, g_q_lat, w_uq, g_kv_lat, w_ukv, w_o_a, w_o_b, w_out, rel_bias, final_g):
    B, S, D = x.shape
    assert D == D_MODEL and S % TILE == 0 and norm_g.shape[0] == 1
    assert MLA_V == HEAD_V and DSA_HEAD_DIM == HEAD_V
    assert S // ROW_CHUNK <= 256
    nb = S // TILE
    topk = min(TOPK_MAX, S // 4)
    bf = MXU_DTYPE
    params = functools.partial(pltpu.CompilerParams, vmem_limit_bytes=VMEM_LIMIT)

    w1, w2t, wuqm_t, wuqs_t, wk, wv_t = _prep_weights(w_in[0], w_uq[0], w_ukv[0])
    cq_t, sq_t, c_k, s_k = _rope_tables(S)
    n1, n2 = w1.shape[1], w2t.shape[0]
    va_rows, vb_rows = MLA_HEADS * V_ROWS, DSA_HEADS * V_ROWS

    tok = lambda c: pl.BlockSpec((1, TILE, c), lambda b, r: (b, r, 0))
    tok_t = lambda c: pl.BlockSpec((1, c, TILE), lambda b, r: (b, 0, r))
    blk_t = lambda c: pl.BlockSpec((1, 1, c, TILE), lambda b, r: (b, r, 0, 0))
    sds = jax.ShapeDtypeStruct
    (qa_t, ka, va_t, za, qb_t, kb, vb_t, zb, qi_t, ki, wi_t, ga, gb) = pl.pallas_call(
        _proj_kernel,
        grid=(B, nb),
        in_specs=[tok(D), _full((1, D)), _full((D, n1)), _full((n2, D)),
                  _full((1, MLA_Q_RANK)), _full((MLA_HEADS * LANES, MLA_Q_RANK)),
                  _full((MLA_HEADS * LANES, MLA_Q_RANK)),
                  _full((1, MLA_KV_RANK)), _full((MLA_KV_RANK, MLA_HEADS * LANES)),
                  _full((MLA_WIDTH, MLA_KV_RANK)),
                  pl.BlockSpec((LANES, TILE), lambda b, r: (0, r)),
                  pl.BlockSpec((LANES, TILE), lambda b, r: (0, r)),
                  pl.BlockSpec((TILE, LANES), lambda b, r: (r, 0)),
                  pl.BlockSpec((TILE, LANES), lambda b, r: (r, 0))],
        out_specs=[tok_t(MLA_HEADS * LANES), tok(MLA_HEADS * LANES), blk_t(va_rows), tok(MLA_WIDTH),
                   tok_t(DSA_WIDTH), tok(DSA_WIDTH), blk_t(vb_rows), tok(DSA_WIDTH),
                   tok_t(IDX_HEADS * IDX_DIM), tok(LANES), tok_t(IDX_HEADS), tok(D), tok(D)],
        out_shape=[sds((B, MLA_HEADS * LANES, S), bf), sds((B, S, MLA_HEADS * LANES), bf),
                   sds((B, nb, va_rows, TILE), bf), sds((B, S, MLA_WIDTH), bf),
                   sds((B, DSA_WIDTH, S), bf), sds((B, S, DSA_WIDTH), bf),
                   sds((B, nb, vb_rows, TILE), bf), sds((B, S, DSA_WIDTH), bf),
                   sds((B, IDX_HEADS * IDX_DIM, S), bf), sds((B, S, LANES), bf),
                   sds((B, IDX_HEADS, S), F32), sds((B, S, D), bf), sds((B, S, D), bf)],
        compiler_params=params(dimension_semantics=("parallel", "parallel")),
        name="proj",
    )(x, norm_g, w1, w2t, g_q_lat, wuqm_t, wuqs_t, g_kv_lat, wk, wv_t, cq_t, sq_t, c_k, s_k)

    def flash_state(heads):
        return ([pltpu.VMEM((heads, 1, TILE), F32),
                 pltpu.VMEM((heads, 2, 1, TILE), F32),
                 pltpu.VMEM((heads, V_ROWS, TILE), F32),
                 pltpu.VMEM((heads * HEAD_V, TILE), F32)]
                + [pltpu.VMEM((2, TILE, TILE), F32)] * heads)

    ya = pl.pallas_call(
        _mla_kernel,
        grid=(B, nb),
        in_specs=[pl.BlockSpec((1, MLA_HEADS * LANES, TILE), lambda b, i: (b, 0, i)),
                  _resident((1, S, MLA_HEADS * LANES), lambda b, i: (b, 0, 0)),
                  _resident((1, nb, va_rows, TILE), lambda b, i: (b, 0, 0, 0)),
                  pl.BlockSpec((1, TILE, MLA_WIDTH), lambda b, i: (b, i, 0))],
        out_specs=pl.BlockSpec((1, TILE, MLA_WIDTH), lambda b, i: (b, i, 0)),
        out_shape=sds((B, S, MLA_WIDTH), bf),
        scratch_shapes=flash_state(MLA_HEADS),
        compiler_params=params(dimension_semantics=("parallel", "arbitrary")),
        name="mla",
    )(qa_t, ka, va_t, za)

    yb = pl.pallas_call(
        functools.partial(_dsa_kernel, topk),
        grid=(B, nb),
        in_specs=[pl.BlockSpec((1, DSA_WIDTH, TILE), lambda b, i: (b, 0, i)),
                  _resident((1, S, DSA_WIDTH), lambda b, i: (b, 0, 0)),
                  _resident((1, nb, vb_rows, TILE), lambda b, i: (b, 0, 0, 0)),
                  pl.BlockSpec((1, IDX_HEADS * IDX_DIM, TILE), lambda b, i: (b, 0, i)),
                  _resident((1, S, LANES), lambda b, i: (b, 0, 0)),
                  pl.BlockSpec((1, IDX_HEADS, TILE), lambda b, i: (b, 0, i)),
                  pl.BlockSpec((1, TILE, DSA_WIDTH), lambda b, i: (b, i, 0)),
                  _resident((DSA_HEADS, 2, TILE, TILE), lambda b, i: (0, 0, 0, 0))],
        out_specs=pl.BlockSpec((1, TILE, DSA_WIDTH), lambda b, i: (b, i, 0)),
        out_shape=sds((B, S, DSA_WIDTH), bf),
        scratch_shapes=[pltpu.VMEM((S, TILE), F32),
                        pltpu.VMEM((S, TILE), jnp.bfloat16),
                        pltpu.VMEM((DSA_HEADS, LANES, TILE), bf),
                        pltpu.VMEM((IDX_HEADS, LANES, TILE), bf),
                        pltpu.VMEM((2, TILE, TILE), F32),
                        pltpu.VMEM((4, 8, TILE), F32),
                        *flash_state(DSA_HEADS)],
        compiler_params=params(dimension_semantics=("parallel", "arbitrary")),
        name="dsa",
    )(qb_t, kb, vb_t, qi_t, ki, wi_t, zb, _bias_tables(rel_bias))

    return pl.pallas_call(
        _merge_kernel,
        grid=(B, nb),
        in_specs=[tok(D), tok(MLA_WIDTH), tok(DSA_WIDTH), tok(D), tok(D),
                  _full((MLA_WIDTH, D)), _full((DSA_WIDTH, D)), _full((D, D)), _full((1, D))],
        out_specs=tok(D),
        out_shape=sds((B, S, D), x.dtype),
        compiler_params=params(dimension_semantics=("parallel", "parallel")),
        name="merge",
    )(x, ya, yb, ga, gb, w_o_a[0].astype(bf), w_o_b[0].astype(bf), w_out[0].astype(bf),
      final_g.reshape(1, D))
```
